```python
import jax
import jax.numpy as jnp
from jax import lax
import numpy as np

D_MODEL = 2048
BATCH = 2
SEQ = 4096
DEPTH = 4
DEC_BATCH = 32
DEC_SEQ = 64
PAST_LEN = 1024

CHUNK = 64
Q_BLOCK = 128
N_MEM = 256
RMS_EPS = 1e-6

FOX_HEAD_DIM = 128
FOX_HEADS = D_MODEL // FOX_HEAD_DIM
FOX_WIDTH = FOX_HEADS * FOX_HEAD_DIM
MLA_HEADS = D_MODEL // 128
MLA_Q_LORA = D_MODEL // 4
MLA_KV_LORA = 512
MLA_NOPE = 128
MLA_ROPE = 64
MLA_V = 128
ROPE_BASE = 10000.0
MEM_HEADS = 4
MEM_HEAD_DIM = 128
MEM_WIDTH = MEM_HEADS * MEM_HEAD_DIM
D_FF = ((8 * D_MODEL // 3 + 255) // 256) * 256
N_EXPERTS = 8
TOP_K = 2

N_FOX = (DEPTH + 1) // 2
N_MLA = DEPTH // 2

kernel_name = 'streaming_fox_mla_hybrid_step'


def rmsnorm(x, g):
    xf = x.astype(jnp.float32)
    y = xf * lax.rsqrt(jnp.mean(xf * xf, axis=-1, keepdims=True) + RMS_EPS)
    return (y * g.astype(jnp.float32)).astype(x.dtype)


def rope(x, pos):
    half = x.shape[-1] // 2
    inv_freq = ROPE_BASE ** (-jnp.arange(half, dtype=jnp.float32) / half)
    ang = pos.astype(jnp.float32)[:, None] * inv_freq[None, :]
    ang = ang.reshape((ang.shape[0],) + (1,) * (x.ndim - 3) + (half,))
    cos, sin = jnp.cos(ang), jnp.sin(ang)
    xf = x.astype(jnp.float32)
    x1, x2 = xf[..., :half], xf[..., half:]
    return jnp.concatenate([x1 * cos - x2 * sin, x2 * cos + x1 * sin], axis=-1).astype(x.dtype)


def attend_block(q, k, v, q_pos, k_pos, unit, scale, cq=None, ck=None):
    if k.ndim == 3:
        s = jnp.einsum('bqhd,bkd->bhqk', q, k)
    else:
        s = jnp.einsum('bqhd,bkhd->bhqk', q, k)
    s = s.astype(jnp.float32) * scale
    if cq is not None:
        s = s + (cq[:, :, :, None] - ck[:, :, None, :])
    allowed = (k_pos[None, :] // unit) <= (q_pos[:, None] // unit)
    s = jnp.where(allowed[None, None], s, -jnp.inf)
    p = jax.nn.softmax(s, axis=-1).astype(v.dtype)
    if v.ndim == 3:
        return jnp.einsum('bhqk,bkd->bqhd', p, v)
    return jnp.einsum('bhqk,bkhd->bqhd', p, v)


def attention(q, k, v, q_pos, k_pos, unit, scale, cq=None, ck=None):
    B, T, H, Dk = q.shape
    if T <= Q_BLOCK:
        return attend_block(q, k, v, q_pos, k_pos, unit, scale, cq, ck)
    n = T // Q_BLOCK
    qb = q.reshape(B, n, Q_BLOCK, H, Dk).swapaxes(0, 1)
    pb = q_pos.reshape(n, Q_BLOCK)
    if cq is None:
        ob = lax.map(lambda xs: attend_block(xs[0], k, v, xs[1], k_pos, unit, scale), (qb, pb))
    else:
        cqb = cq.reshape(B, H, n, Q_BLOCK).transpose(2, 0, 1, 3)
        ob = lax.map(lambda xs: attend_block(xs[0], k, v, xs[1], k_pos, unit, scale, xs[2], ck), (qb, pb, cqb))
    return ob.swapaxes(0, 1).reshape(B, T, H, ob.shape[-1])


def fox_mixer(h, w_in, b_f, w_o, cache=None):
    B, T, _ = h.shape
    a = h @ w_in
    q = a[..., :FOX_WIDTH].reshape(B, T, FOX_HEADS, FOX_HEAD_DIM)
    k = a[..., FOX_WIDTH:2 * FOX_WIDTH].reshape(B, T, FOX_HEADS, FOX_HEAD_DIM)
    v = a[..., 2 * FOX_WIDTH:3 * FOX_WIDTH].reshape(B, T, FOX_HEADS, FOX_HEAD_DIM)
    logf = jax.nn.log_sigmoid((a[..., 3 * FOX_WIDTH:] + b_f).astype(jnp.float32))
    if cache is None:
        past = 0
        k_all, v_all, logf_all = k, v, logf
    else:
        c_k, c_v, c_lf = cache
        past = c_k.shape[1]
        k_all = jnp.concatenate([c_k, k], axis=1)
        v_all = jnp.concatenate([c_v, v], axis=1)
        logf_all = jnp.concatenate([c_lf.astype(jnp.float32), logf], axis=1)
    S = k_all.shape[1]
    csum = jnp.cumsum(logf_all, axis=1).transpose(0, 2, 1)
    o = attention(q, k_all, v_all, past + jnp.arange(T), jnp.arange(S), 1,
                  FOX_HEAD_DIM ** -0.5, csum[:, :, past:], csum)
    return o.reshape(B, T, FOX_WIDTH) @ w_o, k, v, logf


def mla_mixer(h, w_in, g_q, g_kv, w_uq, w_ukv, w_o, cache=None):
    B, T, _ = h.shape
    past = 0 if cache is None else cache[0].shape[1]
    pos = past + jnp.arange(T)
    a = h @ w_in
    c_q = rmsnorm(a[..., :MLA_Q_LORA], g_q)
    c_kv = rmsnorm(a[..., MLA_Q_LORA:MLA_Q_LORA + MLA_KV_LORA], g_kv)
    k_r = rope(a[..., MLA_Q_LORA + MLA_KV_LORA:], pos)
    q = (c_q @ w_uq).reshape(B, T, MLA_HEADS, MLA_NOPE + MLA_ROPE)
    q_nope, q_rope = q[..., :MLA_NOPE], rope(q[..., MLA_NOPE:], pos)
    scale = (MLA_NOPE + MLA_ROPE) ** -0.5
    w_ukv_h = w_ukv.reshape(MLA_KV_LORA, MLA_HEADS, MLA_NOPE + MLA_V)
    if cache is None:
        kv = jnp.einsum('bsc,chd->bshd', c_kv, w_ukv_h)
        k = jnp.concatenate([kv[..., :MLA_NOPE],
                             jnp.broadcast_to(k_r[:, :, None, :], (B, T, MLA_HEADS, MLA_ROPE))], axis=-1)
        qf = jnp.concatenate([q_nope, q_rope], axis=-1)
        o = attention(qf, k, kv[..., MLA_NOPE:], pos, pos, CHUNK, scale)
    else:
        ckv_all = jnp.concatenate([cache[0], c_kv], axis=1)
        kr_all = jnp.concatenate([cache[1], k_r], axis=1)
        S = ckv_all.shape[1]
        q_lat = jnp.einsum('bqhn,chn->bqhc', q_nope, w_ukv_h[..., :MLA_NOPE])
        qa = jnp.concatenate([q_lat, q_rope], axis=-1)
        ka = jnp.concatenate([ckv_all, kr_all], axis=-1)
        o_lat = attention(qa, ka, ckv_all, pos, jnp.arange(S), CHUNK, scale)
        o = jnp.einsum('bqhc,chv->bqhv', o_lat, w_ukv_h[..., MLA_NOPE:])
    return o.reshape(B, T, MLA_HEADS * MLA_V) @ w_o, c_kv, k_r


def memory_kv(mem, w_kv):
    B, M, _ = mem.shape
    kv = (mem @ w_kv).reshape(B, M, 2, MEM_HEADS, MEM_HEAD_DIM)
    return kv[:, :, 0], kv[:, :, 1]


def cross_attention(h, mem_k, mem_v, w_q, w_o):
    B, T, _ = h.shape
    q = (h @ w_q).reshape(B, T, MEM_HEADS, MEM_HEAD_DIM)
    s = jnp.einsum('bqhd,bmhd->bhqm', q, mem_k).astype(jnp.float32) * MEM_HEAD_DIM ** -0.5
    p = jax.nn.softmax(s, axis=-1).astype(mem_v.dtype)
    o = jnp.einsum('bhqm,bmhd->bqhd', p, mem_v)
    return o.reshape(B, T, MEM_WIDTH) @ w_o


def swiglu(x, w_gu, w_d):
    gu = x @ w_gu
    return (jax.nn.silu(gu[..., :D_FF]) * gu[..., D_FF:]) @ w_d


def moe_ffn(h, w_router, w_gu, w_d):
    B, T, D = h.shape
    xt = h.reshape(B * T, D)
    logits = (xt @ w_router).astype(jnp.float32)
    top_val, top_idx = lax.top_k(logits, TOP_K)
    top_w = jax.nn.softmax(top_val, axis=-1)
    gates = jnp.sum(jax.nn.one_hot(top_idx, N_EXPERTS, dtype=jnp.float32) * top_w[..., None], axis=1)
    out = jnp.zeros_like(xt)
    for e in range(N_EXPERTS):
        out = out + gates[:, e:e + 1].astype(xt.dtype) * swiglu(xt, w_gu[e], w_d[e])
    return out.reshape(B, T, D)


def run_trunk(x, mem_k, mem_v, w, fox_cache=None, mla_cache=None):
    fox_rows, mla_rows = [], []
    for i in range(DEPTH):
        j = i // 2
        h = rmsnorm(x, w['norm_mix'][i])
        if i % 2 == 0:
            cache = None if fox_cache is None else tuple(c[j] for c in fox_cache)
            y, k, v, lf = fox_mixer(h, w['fox_w_in'][j], w['fox_b_f'][j], w['fox_w_o'][j], cache)
            fox_rows.append((k, v, lf))
        else:
            cache = None if mla_cache is None else tuple(c[j] for c in mla_cache)
            y, ckv, kr = mla_mixer(h, w['mla_w_in'][j], w['mla_g_q'][j], w['mla_g_kv'][j],
                                   w['mla_w_uq'][j], w['mla_w_ukv'][j], w['mla_w_o'][j], cache)
            mla_rows.append((ckv, kr))
        x = x + y
        h = rmsnorm(x, w['norm_cross'][i])
        x = x + cross_attention(h, mem_k[i], mem_v[i], w['cross_w_q'][i], w['cross_w_o'][i])
        h = rmsnorm(x, w['norm_ffn'][i])
        if i % 2 == 0:
            x = x + swiglu(h, w['ffn_w_gu'][j], w['ffn_w_d'][j])
        else:
            x = x + moe_ffn(h, w['moe_w_router'][j], w['moe_w_gu'][j], w['moe_w_d'][j])
    y = rmsnorm(x, w['norm_final'])
    fox_state = tuple(jnp.stack([r[n] for r in fox_rows]) for n in range(3))
    mla_state = tuple(jnp.stack([r[n] for r in mla_rows]) for n in range(2))
    return y, fox_state, mla_state


def setup_inputs(seed: int = 0) -> dict:
    key = jax.random.key(seed)
    keys = iter(jax.random.split(key, 40))

    def nrm(shape, scale=1.0):
        return jax.random.normal(next(keys), shape, jnp.float32) * scale

    def gain(shape):
        return 1.0 + 0.02 * jax.random.normal(next(keys), shape, jnp.float32)

    D = D_MODEL
    return {
        'x_prompt': nrm((BATCH, SEQ, D)),
        'x_sample': nrm((DEC_BATCH, DEC_SEQ, D)),
        'cache_fox_k': nrm((N_FOX, DEC_BATCH, PAST_LEN, FOX_HEADS, FOX_HEAD_DIM)),
        'cache_fox_v': nrm((N_FOX, DEC_BATCH, PAST_LEN, FOX_HEADS, FOX_HEAD_DIM)),
        'cache_fox_logf': jax.nn.log_sigmoid(nrm((N_FOX, DEC_BATCH, PAST_LEN, FOX_HEADS))),
        'cache_mla_ckv': nrm((N_MLA, DEC_BATCH, PAST_LEN, MLA_KV_LORA)),
        'cache_mla_krope': nrm((N_MLA, DEC_BATCH, PAST_LEN, MLA_ROPE)),
        'cache_mem_k': nrm((DEPTH, DEC_BATCH, N_MEM, MEM_HEADS, MEM_HEAD_DIM)),
        'cache_mem_v': nrm((DEPTH, DEC_BATCH, N_MEM, MEM_HEADS, MEM_HEAD_DIM)),
        'mem_prompt': nrm((BATCH, N_MEM, D)),
        'norm_mix': gain((DEPTH, D)),
        'norm_cross': gain((DEPTH, D)),
        'norm_ffn': gain((DEPTH, D)),
        'norm_final': gain((D,)),
        'fox_w_in': nrm((N_FOX, D, 3 * FOX_WIDTH + FOX_HEADS), D ** -0.5),
        'fox_b_f': nrm((N_FOX, FOX_HEADS), 0.1),
        'fox_w_o': nrm((N_FOX, FOX_WIDTH, D), FOX_WIDTH ** -0.5),
        'mla_w_in': nrm((N_MLA, D, MLA_Q_LORA + MLA_KV_LORA + MLA_ROPE), D ** -0.5),
        'mla_g_q': gain((N_MLA, MLA_Q_LORA)),
        'mla_g_kv': gain((N_MLA, MLA_KV_LORA)),
        'mla_w_uq': nrm((N_MLA, MLA_Q_LORA, MLA_HEADS * (MLA_NOPE + MLA_ROPE)), MLA_Q_LORA ** -0.5),
        'mla_w_ukv': nrm((N_MLA, MLA_KV_LORA, MLA_HEADS * (MLA_NOPE + MLA_V)), MLA_KV_LORA ** -0.5),
        'mla_w_o': nrm((N_MLA, MLA_HEADS * MLA_V, D), (MLA_HEADS * MLA_V) ** -0.5),
        'cross_w_q': nrm((DEPTH, D, MEM_WIDTH), D ** -0.5),
        'cross_w_kv': nrm((DEPTH, D, 2 * MEM_WIDTH), D ** -0.5),
        'cross_w_o': nrm((DEPTH, MEM_WIDTH, D), MEM_WIDTH ** -0.5),
        'ffn_w_gu': nrm((N_FOX, D, 2 * D_FF), D ** -0.5),
        'ffn_w_d': nrm((N_FOX, D_FF, D), D_FF ** -0.5),
        'moe_w_router': nrm((N_MLA, D, N_EXPERTS), D ** -0.5),
        'moe_w_gu': nrm((N_MLA, N_EXPERTS, D, 2 * D_FF), D ** -0.5),
        'moe_w_d': nrm((N_MLA, N_EXPERTS, D_FF, D), D_FF ** -0.5),
    }


def reference(x_prompt, x_sample, cache_fox_k, cache_fox_v, cache_fox_logf, cache_mla_ckv,
              cache_mla_krope, cache_mem_k, cache_mem_v, mem_prompt,
              norm_mix, norm_cross, norm_ffn, norm_final,
              fox_w_in, fox_b_f, fox_w_o,
              mla_w_in, mla_g_q, mla_g_kv, mla_w_uq, mla_w_ukv, mla_w_o,
              cross_w_q, cross_w_kv, cross_w_o,
              ffn_w_gu, ffn_w_d, moe_w_router, moe_w_gu, moe_w_d):
    w = {
        'norm_mix': norm_mix, 'norm_cross': norm_cross, 'norm_ffn': norm_ffn, 'norm_final': norm_final,
        'fox_w_in': fox_w_in, 'fox_b_f': fox_b_f, 'fox_w_o': fox_w_o,
        'mla_w_in': mla_w_in, 'mla_g_q': mla_g_q, 'mla_g_kv': mla_g_kv,
        'mla_w_uq': mla_w_uq, 'mla_w_ukv': mla_w_ukv, 'mla_w_o': mla_w_o,
        'cross_w_q': cross_w_q, 'cross_w_o': cross_w_o,
        'ffn_w_gu': ffn_w_gu, 'ffn_w_d': ffn_w_d,
        'moe_w_router': moe_w_router, 'moe_w_gu': moe_w_gu, 'moe_w_d': moe_w_d,
    }
    mem_kv = [memory_kv(mem_prompt, cross_w_kv[i]) for i in range(DEPTH)]
    mem_k_p = jnp.stack([kv[0] for kv in mem_kv])
    mem_v_p = jnp.stack([kv[1] for kv in mem_kv])
    y_prompt, fox_p, mla_p = run_trunk(x_prompt, mem_k_p, mem_v_p, w)
    y_sample, fox_s, mla_s = run_trunk(x_sample, cache_mem_k, cache_mem_v, w,
                                       (cache_fox_k, cache_fox_v, cache_fox_logf),
                                       (cache_mla_ckv, cache_mla_krope))
    fox_k_p, fox_v_p, fox_logf_p = fox_p
    mla_ckv_p, mla_kr_p = mla_p
    fox_k_s, fox_v_s, fox_logf_s = fox_s
    mla_ckv_s, mla_kr_s = mla_s
    return (y_prompt, y_sample, fox_k_p, fox_v_p, fox_logf_p, mla_ckv_p, mla_kr_p, mem_k_p, mem_v_p,
            fox_k_s, fox_v_s, fox_logf_s, mla_ckv_s, mla_kr_s)
```

```python
import functools

import numpy as np
import jax
import jax.numpy as jnp
from jax import lax
from jax.experimental import pallas as pl
from jax.experimental.pallas import tpu as pltpu

F32 = jnp.float32
BF16 = jnp.bfloat16

D_MODEL = 2048
CHUNK = 64
RMS_EPS = 1e-6
FOX_HEADS = 16
HEAD_DIM = 128
MLA_HEADS = 16
MLA_Q_LORA = 512
MLA_KV_LORA = 512
MLA_NOPE = 128
MLA_ROPE = 64
MLA_V = 128
ROPE_BASE = 10000.0
MEM_HEADS = 4
N_EXPERTS = 8
LANES = 128

VMEM_LIMIT_BYTES = 56 * 1024 * 1024
NEG_BIG = -1e30


def _cparams(sem):
    return pltpu.CompilerParams(dimension_semantics=sem, vmem_limit_bytes=VMEM_LIMIT_BYTES)


def _rms(xf, g):
    return xf * lax.rsqrt(jnp.mean(xf * xf, axis=-1, keepdims=True) + RMS_EPS) * g


def _log_sigmoid(x):
    return jnp.minimum(x, 0.0) - jnp.log1p(jnp.exp(-jnp.abs(x)))


def _mm_body(*refs, has_norm, has_bias, has_res, act, use_scratch, n_out):
    it = iter(refs)
    x_ref = next(it)
    g_ref = next(it) if has_norm else None
    w_ref = next(it)
    b_ref = next(it) if has_bias else None
    r_ref = next(it) if has_res else None
    out_refs = [next(it) for _ in range(n_out)]
    xb_ref = next(it) if use_scratch else None

    if use_scratch:
        @pl.when(pl.program_id(1) == 0)
        def _():
            xf = x_ref[...].astype(F32)
            if has_norm:
                xf = _rms(xf, g_ref[...])
            xb_ref[...] = xf.astype(BF16)
        xb = xb_ref[...]
    else:
        xb = x_ref[...]
    acc = jnp.dot(xb, w_ref[...].astype(BF16), preferred_element_type=F32)
    if has_bias:
        acc = acc + b_ref[...]
    if act == "log_sigmoid":
        acc = _log_sigmoid(acc)
    if has_res:
        acc = acc + r_ref[...]
    for o in out_refs:
        o[...] = acc.astype(o.dtype)


def _matmul(x, w, *, gain=None, bias=None, res=None, act=None, x_col=0, k=None,
            w_col0=0, m=None, n_rows=None, out_dtypes=(F32,), tm=1024, tn=512):
    n = x.shape[0] if n_rows is None else n_rows
    k = x.shape[1] if k is None else k
    m = w.shape[1] - w_col0 if m is None else m
    tm = min(tm, n)
    tn = min(tn, m)
    assert n % tm == 0 and m % tn == 0 and w_col0 % tn == 0 and w.shape[0] == k
    has_norm, has_bias, has_res = gain is not None, bias is not None, res is not None
    use_scratch = has_norm or x.dtype != BF16
    col0 = w_col0 // tn
    in_specs = [pl.BlockSpec((tm, k), lambda i, j: (i, x_col))]
    args = [x]
    if has_norm:
        in_specs.append(pl.BlockSpec((1, k), lambda i, j: (0, 0)))
        args.append(gain.reshape(1, k).astype(F32))
    in_specs.append(pl.BlockSpec((k, tn), lambda i, j: (0, j + col0)))
    args.append(w)
    if has_bias:
        in_specs.append(pl.BlockSpec((1, tn), lambda i, j: (0, j)))
        args.append(bias.reshape(1, m).astype(F32))
    if has_res:
        in_specs.append(pl.BlockSpec((tm, tn), lambda i, j: (i, j)))
        args.append(res)
    body = functools.partial(_mm_body, has_norm=has_norm, has_bias=has_bias, has_res=has_res,
                             act=act, use_scratch=use_scratch, n_out=len(out_dtypes))
    outs = pl.pallas_call(
        body,
        grid=(n // tm, m // tn),
        in_specs=in_specs,
        out_specs=[pl.BlockSpec((tm, tn), lambda i, j: (i, j)) for _ in out_dtypes],
        out_shape=[jax.ShapeDtypeStruct((n, m), dt) for dt in out_dtypes],
        scratch_shapes=[pltpu.VMEM((tm, k), BF16)] if use_scratch else [],
        compiler_params=_cparams(("parallel", "arbitrary")),
        name="matmul",
    )(*args)
    return outs[0] if len(out_dtypes) == 1 else outs


def _norm_body(x_ref, g_ref, *out_refs):
    y = _rms(x_ref[...].astype(F32), g_ref[...])
    for o in out_refs:
        o[...] = y.astype(o.dtype)


def _rmsnorm(x, gain, *, x_col=0, k=None, out_dtypes=(F32,), tm=1024):
    n = x.shape[0]
    k = x.shape[1] if k is None else k
    tm = min(tm, n)
    assert n % tm == 0
    outs = pl.pallas_call(
        _norm_body,
        grid=(n // tm,),
        in_specs=[pl.BlockSpec((tm, k), lambda i: (i, x_col)),
                  pl.BlockSpec((1, k), lambda i: (0, 0))],
        out_specs=[pl.BlockSpec((tm, k), lambda i: (i, 0)) for _ in out_dtypes],
        out_shape=[jax.ShapeDtypeStruct((n, k), dt) for dt in out_dtypes],
        compiler_params=_cparams(("parallel",)),
        name="rmsnorm",
    )(x, gain.reshape(1, k).astype(F32))
    return outs[0] if len(out_dtypes) == 1 else outs


def _swiglu_partial(xb, wg, wu, wd):
    g = jnp.dot(xb, wg.astype(BF16), preferred_element_type=F32)
    u = jnp.dot(xb, wu.astype(BF16), preferred_element_type=F32)
    a = (g * jax.nn.sigmoid(g)) * u
    return jnp.dot(a.astype(BF16), wd.astype(BF16), preferred_element_type=F32)


def _ffn_body(x_ref, g_ref, wg_ref, wu_ref, wd_ref, out_ref, xb_ref):
    @pl.when(pl.program_id(1) == 0)
    def _():
        xf = x_ref[...]
        xb_ref[...] = _rms(xf, g_ref[...]).astype(BF16)
        out_ref[...] = xf
    out_ref[...] += _swiglu_partial(xb_ref[...], wg_ref[...], wu_ref[...], wd_ref[...])


def _ffn(x, gain, w_gu, w_d, *, tm=1024, tf=256):
    n, d = x.shape
    f = w_d.shape[0]
    assert n % tm == 0 and f % tf == 0
    nf = f // tf
    return pl.pallas_call(
        _ffn_body,
        grid=(n // tm, nf),
        in_specs=[pl.BlockSpec((tm, d), lambda i, j: (i, 0)),
                  pl.BlockSpec((1, d), lambda i, j: (0, 0)),
                  pl.BlockSpec((d, tf), lambda i, j: (0, j)),
                  pl.BlockSpec((d, tf), lambda i, j: (0, j + nf)),
                  pl.BlockSpec((tf, d), lambda i, j: (j, 0))],
        out_specs=pl.BlockSpec((tm, d), lambda i, j: (i, 0)),
        out_shape=jax.ShapeDtypeStruct((n, d), F32),
        scratch_shapes=[pltpu.VMEM((tm, d), BF16)],
        compiler_params=_cparams(("parallel", "arbitrary")),
        name="ffn",
    )(x, gain.reshape(1, d).astype(F32), w_gu, w_gu, w_d)


def _cumsum_body(x_ref, out_ref, *, chunk):
    t = x_ref.shape[1]
    row = lax.broadcasted_iota(jnp.int32, (chunk, chunk), 0)
    col = lax.broadcasted_iota(jnp.int32, (chunk, chunk), 1)
    tri = jnp.where(col <= row, 1.0, 0.0).astype(BF16)
    carry = jnp.zeros((1, x_ref.shape[2]), F32)
    for c in range(t // chunk):
        xc = x_ref[0, c * chunk:(c + 1) * chunk, :]
        hi = xc.astype(BF16)
        r1 = xc - hi.astype(F32)
        mid = r1.astype(BF16)
        lo = (r1 - mid.astype(F32)).astype(BF16)
        cs = (jnp.dot(tri, hi, preferred_element_type=F32)
              + jnp.dot(tri, mid, preferred_element_type=F32)
              + jnp.dot(tri, lo, preferred_element_type=F32)) + carry
        out_ref[0, c * chunk:(c + 1) * chunk, :] = cs
        carry = cs[chunk - 1:chunk, :]


def _cumsum_time(x):
    b, t, h = x.shape
    chunk = next(c for c in (256, 128, 64, 32, 16, 8) if t % c == 0)
    return pl.pallas_call(
        functools.partial(_cumsum_body, chunk=chunk),
        grid=(b,),
        in_specs=[pl.BlockSpec((1, t, h), lambda i: (i, 0, 0))],
        out_specs=pl.BlockSpec((1, t, h), lambda i: (i, 0, 0)),
        out_shape=jax.ShapeDtypeStruct((b, t, h), F32),
        compiler_params=_cparams(("parallel",)),
        name="cumsum",
    )(x)


def _qk(q, k):
    return lax.dot_general(q, k, (((1,), (1,)), ((), ())), preferred_element_type=F32)


def _flash_step(s, v, m_ref, l_ref, acc_ref, col, lanes):
    m_old = m_ref[:, col:col + 1]
    m_new = jnp.maximum(m_old, jnp.max(s, axis=1, keepdims=True))
    alpha = jnp.exp(m_old - m_new)
    p = jnp.exp(s - m_new)
    l_ref[:, col:col + 1] = alpha * l_ref[:, col:col + 1] + jnp.sum(p, axis=1, keepdims=True)
    acc_ref[:, lanes] = alpha * acc_ref[:, lanes] + jnp.dot(p.astype(BF16), v, preferred_element_type=F32)
    m_ref[:, col:col + 1] = m_new


def _flash_init(m_ref, l_ref, acc_ref):
    m_ref[...] = jnp.full(m_ref.shape, NEG_BIG, F32)
    l_ref[...] = jnp.zeros(l_ref.shape, F32)
    acc_ref[...] = jnp.zeros(acc_ref.shape, F32)


def _fox_prompt_body(q_ref, k_ref, v_ref, cq_ref, ck_ref, o_ref, m_ref, l_ref, acc_ref, *, scale, t):
    qi, ki = pl.program_id(1), pl.program_id(2)

    @pl.when(ki == 0)
    def _():
        _flash_init(m_ref, l_ref, acc_ref)

    @pl.when(ki <= qi)
    def _():
        row = lax.broadcasted_iota(jnp.int32, (t, t), 0) + qi * t
        col = lax.broadcasted_iota(jnp.int32, (t, t), 1) + ki * t
        allowed = col <= row
        for h in range(FOX_HEADS):
            sl = slice(h * HEAD_DIM, (h + 1) * HEAD_DIM)
            s = _qk(q_ref[:, sl], k_ref[:, sl]) * scale
            s = s + (cq_ref[0, :, h:h + 1] - ck_ref[0, h:h + 1, :])
            s = jnp.where(allowed, s, NEG_BIG)
            _flash_step(s, v_ref[:, sl], m_ref, l_ref, acc_ref, h, sl)

    @pl.when(ki == qi)
    def _():
        for h in range(FOX_HEADS):
            sl = slice(h * HEAD_DIM, (h + 1) * HEAD_DIM)
            o_ref[:, sl] = (acc_ref[:, sl] / l_ref[:, h:h + 1]).astype(o_ref.dtype)


def _fox_attn_prompt(qkv, csum, csum_t, *, batch, seq, t=512):
    w = FOX_HEADS * HEAD_DIM
    nt = seq // t
    assert seq % t == 0
    return pl.pallas_call(
        functools.partial(_fox_prompt_body, scale=HEAD_DIM ** -0.5, t=t),
        grid=(batch, nt, nt),
        in_specs=[pl.BlockSpec((t, w), lambda b, i, j: (b * nt + i, 0)),
                  pl.BlockSpec((t, w), lambda b, i, j: (b * nt + jnp.minimum(i, j), 1)),
                  pl.BlockSpec((t, w), lambda b, i, j: (b * nt + jnp.minimum(i, j), 2)),
                  pl.BlockSpec((1, t, FOX_HEADS), lambda b, i, j: (b, i, 0)),
                  pl.BlockSpec((1, FOX_HEADS, t), lambda b, i, j: (b, 0, jnp.minimum(i, j)))],
        out_specs=pl.BlockSpec((t, w), lambda b, i, j: (b * nt + i, 0)),
        out_shape=jax.ShapeDtypeStruct((batch * seq, w), BF16),
        scratch_shapes=[pltpu.VMEM((t, LANES), F32), pltpu.VMEM((t, LANES), F32),
                        pltpu.VMEM((t, w), F32)],
        compiler_params=_cparams(("parallel", "parallel", "arbitrary")),
        name="fox_attn_prompt",
    )(qkv, qkv, qkv, csum, csum_t)


def _fox_sample_body(q_ref, kn_ref, vn_ref, kc_ref, vc_ref, cq_ref, ck_ref, o_ref, *, scale, past, tq):
    row = lax.broadcasted_iota(jnp.int32, (tq, tq), 0)
    col = lax.broadcasted_iota(jnp.int32, (tq, tq), 1)
    allowed = col <= row
    for h in range(FOX_HEADS):
        sl = slice(h * HEAD_DIM, (h + 1) * HEAD_DIM)
        q = q_ref[:, sl]
        cq = cq_ref[0, :, h:h + 1]
        kc = kc_ref[0, 0, :, h, :].astype(BF16)
        vc = vc_ref[0, 0, :, h, :].astype(BF16)
        s_c = _qk(q, kc) * scale + (cq - ck_ref[0, h:h + 1, :past])
        s_n = _qk(q, kn_ref[:, sl]) * scale + (cq - ck_ref[0, h:h + 1, past:])
        s_n = jnp.where(allowed, s_n, NEG_BIG)
        m = jnp.maximum(jnp.max(s_c, axis=1, keepdims=True), jnp.max(s_n, axis=1, keepdims=True))
        p_c = jnp.exp(s_c - m)
        p_n = jnp.exp(s_n - m)
        l = jnp.sum(p_c, axis=1, keepdims=True) + jnp.sum(p_n, axis=1, keepdims=True)
        o = (jnp.dot(p_c.astype(BF16), vc, preferred_element_type=F32)
             + jnp.dot(p_n.astype(BF16), vn_ref[:, sl], preferred_element_type=F32))
        o_ref[:, sl] = (o / l).astype(o_ref.dtype)


def _fox_attn_sample(qkv, cache_k, cache_v, layer, csum_q, csum_t, *, row0, batch, tq):
    w = FOX_HEADS * HEAD_DIM
    past = cache_k.shape[2]
    rb0 = row0 // tq
    cache_spec = pl.BlockSpec((1, 1, past, FOX_HEADS, HEAD_DIM), lambda b: (layer, b, 0, 0, 0))
    return pl.pallas_call(
        functools.partial(_fox_sample_body, scale=HEAD_DIM ** -0.5, past=past, tq=tq),
        grid=(batch,),
        in_specs=[pl.BlockSpec((tq, w), lambda b: (rb0 + b, 0)),
                  pl.BlockSpec((tq, w), lambda b: (rb0 + b, 1)),
                  pl.BlockSpec((tq, w), lambda b: (rb0 + b, 2)),
                  cache_spec, cache_spec,
                  pl.BlockSpec((1, tq, FOX_HEADS), lambda b: (b, 0, 0)),
                  pl.BlockSpec((1, FOX_HEADS, past + tq), lambda b: (b, 0, 0))],
        out_specs=pl.BlockSpec((tq, w), lambda b: (b, 0)),
        out_shape=jax.ShapeDtypeStruct((batch * tq, w), BF16),
        compiler_params=_cparams(("parallel",)),
        name="fox_attn_sample",
    )(qkv, qkv, qkv, cache_k, cache_v, csum_q, csum_t)


def _rope_body(x_ref, cos_ref, sin_ref, *out_refs, width):
    cos, sin = cos_ref[...], sin_ref[...]
    for h in range(width // LANES):
        a = x_ref[:, h * LANES:(h + 1) * LANES]
        b = x_ref[:, width + h * LANES:width + (h + 1) * LANES]
        y = a * cos + b * sin
        for o in out_refs:
            o[:, h * LANES:(h + 1) * LANES] = y.astype(o.dtype)


def _rope(x, cos, sin, *, x_col=0, width, out_dtypes, tm=512):
    n = x.shape[0]
    assert n % tm == 0
    outs = pl.pallas_call(
        functools.partial(_rope_body, width=width),
        grid=(n // tm,),
        in_specs=[pl.BlockSpec((tm, 2 * width), lambda i: (i, x_col)),
                  pl.BlockSpec((tm, LANES), lambda i: (i, 0)),
                  pl.BlockSpec((tm, LANES), lambda i: (i, 0))],
        out_specs=[pl.BlockSpec((tm, width), lambda i: (i, 0)) for _ in out_dtypes],
        out_shape=[jax.ShapeDtypeStruct((n, width), dt) for dt in out_dtypes],
        compiler_params=_cparams(("parallel",)),
        name="rope",
    )(x, cos, sin)
    return outs[0] if len(out_dtypes) == 1 else outs


def _rope_tables(pos):
    half = MLA_ROPE // 2
    inv_freq = ROPE_BASE ** (-np.arange(half, dtype=np.float32) / half)
    ang = jnp.asarray(pos, F32)[:, None] * jnp.asarray(inv_freq)[None, :]
    zeros = jnp.zeros((ang.shape[0], LANES - MLA_ROPE), F32)
    cos = jnp.concatenate([jnp.cos(ang), jnp.cos(ang), zeros], axis=1)
    sin = jnp.concatenate([jnp.sin(ang), jnp.sin(ang), zeros], axis=1)
    return cos, sin


def _swap_halves(w):
    half = w.shape[-1] // 2
    return jnp.concatenate([-w[..., half:], w[..., :half]], axis=-1)


def _pad_lanes(w):
    return jnp.concatenate([w, jnp.zeros(w.shape[:-1] + (LANES - w.shape[-1],), w.dtype)], axis=-1)


def _mla_prompt_body(qn_ref, qr_ref, kn_ref, v_ref, kr_ref, o_ref, m_ref, l_ref, acc_ref, *, scale, t):
    qi, ki = pl.program_id(1), pl.program_id(2)

    @pl.when(ki == 0)
    def _():
        _flash_init(m_ref, l_ref, acc_ref)

    @pl.when(ki <= qi)
    def _():
        row = lax.broadcasted_iota(jnp.int32, (t, t), 0) + qi * t
        col = lax.broadcasted_iota(jnp.int32, (t, t), 1) + ki * t
        allowed = (col // CHUNK) <= (row // CHUNK)
        kr = kr_ref[...]
        for h in range(MLA_HEADS):
            sl = slice(h * LANES, (h + 1) * LANES)
            q = jnp.concatenate([qn_ref[:, sl], qr_ref[:, sl]], axis=1)
            k = jnp.concatenate([kn_ref[:, sl], kr], axis=1)
            s = jnp.where(allowed, _qk(q, k) * scale, NEG_BIG)
            _flash_step(s, v_ref[:, sl], m_ref, l_ref, acc_ref, h, sl)

    @pl.when(ki == qi)
    def _():
        for h in range(MLA_HEADS):
            sl = slice(h * LANES, (h + 1) * LANES)
            o_ref[:, sl] = (acc_ref[:, sl] / l_ref[:, h:h + 1]).astype(o_ref.dtype)


def _mla_attn_prompt(qn, qr, kv, kr, *, batch, seq, t=512):
    w = MLA_HEADS * LANES
    nt = seq // t
    assert seq % t == 0 and t % CHUNK == 0
    kv_row = lambda b, i, j: b * nt + jnp.minimum(i, j)
    return pl.pallas_call(
        functools.partial(_mla_prompt_body, scale=(MLA_NOPE + MLA_ROPE) ** -0.5, t=t),
        grid=(batch, nt, nt),
        in_specs=[pl.BlockSpec((t, w), lambda b, i, j: (b * nt + i, 0)),
                  pl.BlockSpec((t, w), lambda b, i, j: (b * nt + i, 0)),
                  pl.BlockSpec((t, w), lambda b, i, j: (kv_row(b, i, j), 0)),
                  pl.BlockSpec((t, w), lambda b, i, j: (kv_row(b, i, j), 1)),
                  pl.BlockSpec((t, LANES), lambda b, i, j: (kv_row(b, i, j), 0))],
        out_specs=pl.BlockSpec((t, w), lambda b, i, j: (b * nt + i, 0)),
        out_shape=jax.ShapeDtypeStruct((batch * seq, w), BF16),
        scratch_shapes=[pltpu.VMEM((t, LANES), F32), pltpu.VMEM((t, LANES), F32),
                        pltpu.VMEM((t, w), F32)],
        compiler_params=_cparams(("parallel", "parallel", "arbitrary")),
        name="mla_attn_prompt",
    )(qn, qr, kv, kv, kr)


def _mla_sample_body(qn_ref, qr_ref, cn_ref, krn_ref, cc_ref, krc_ref, wuk_ref, wuv_ref, o_ref,
                     *, scale, past, tq):
    heads = MLA_HEADS
    q_lat = jnp.concatenate(
        [jnp.dot(qn_ref[:, h * LANES:(h + 1) * LANES], wuk_ref[h], preferred_element_type=F32)
         for h in range(heads)], axis=0).astype(BF16)
    q_r = jnp.concatenate([qr_ref[:, h * LANES:(h + 1) * LANES] for h in range(heads)], axis=0)
    c_c = cc_ref[0, 0].astype(BF16)
    kr_c = krc_ref[0, 0].astype(BF16)
    c_n = cn_ref[...]
    s_c = (_qk(q_lat, c_c) + _qk(q_r[:, :MLA_ROPE], kr_c)) * scale
    s_n = (_qk(q_lat, c_n) + _qk(q_r, krn_ref[...])) * scale
    vis_c = (np.arange(past)[None, :] // CHUNK) <= ((past + np.arange(tq))[:, None] // CHUNK)
    vis_n = ((past + np.arange(tq))[None, :] // CHUNK) <= ((past + np.arange(tq))[:, None] // CHUNK)
    if not vis_c.all():
        frame = lax.broadcasted_iota(jnp.int32, s_c.shape, 0) % tq
        key = lax.broadcasted_iota(jnp.int32, s_c.shape, 1)
        s_c = jnp.where((key // CHUNK) <= ((past + frame) // CHUNK), s_c, NEG_BIG)
    if not vis_n.all():
        frame = lax.broadcasted_iota(jnp.int32, s_n.shape, 0) % tq
        key = lax.broadcasted_iota(jnp.int32, s_n.shape, 1)
        s_n = jnp.where(((past + key) // CHUNK) <= ((past + frame) // CHUNK), s_n, NEG_BIG)
    m = jnp.maximum(jnp.max(s_c, axis=1, keepdims=True), jnp.max(s_n, axis=1, keepdims=True))
    p_c = jnp.exp(s_c - m)
    p_n = jnp.exp(s_n - m)
    l = jnp.sum(p_c, axis=1, keepdims=True) + jnp.sum(p_n, axis=1, keepdims=True)
    o_lat = (jnp.dot(p_c.astype(BF16), c_c, preferred_element_type=F32)
             + jnp.dot(p_n.astype(BF16), c_n, preferred_element_type=F32)) / l
    o_lat = o_lat.astype(BF16)
    for h in range(heads):
        o_ref[:, h * LANES:(h + 1) * LANES] = jnp.dot(
            o_lat[h * tq:(h + 1) * tq], wuv_ref[h], preferred_element_type=F32).astype(o_ref.dtype)


def _mla_attn_sample(qn, qr, ckv_new, kr_new, cache_ckv, cache_kr, layer, w_uk_t, w_uv, *, row0, batch, tq):
    w = MLA_HEADS * LANES
    past = cache_ckv.shape[2]
    rb0 = row0 // tq
    return pl.pallas_call(
        functools.partial(_mla_sample_body, scale=(MLA_NOPE + MLA_ROPE) ** -0.5, past=past, tq=tq),
        grid=(batch,),
        in_specs=[pl.BlockSpec((tq, w), lambda b: (rb0 + b, 0)),
                  pl.BlockSpec((tq, w), lambda b: (rb0 + b, 0)),
                  pl.BlockSpec((tq, MLA_KV_LORA), lambda b: (rb0 + b, 0)),
                  pl.BlockSpec((tq, LANES), lambda b: (rb0 + b, 0)),
                  pl.BlockSpec((1, 1, past, MLA_KV_LORA), lambda b: (layer, b, 0, 0)),
                  pl.BlockSpec((1, 1, past, MLA_ROPE), lambda b: (layer, b, 0, 0)),
                  pl.BlockSpec((MLA_HEADS, MLA_NOPE, MLA_KV_LORA), lambda b: (0, 0, 0)),
                  pl.BlockSpec((MLA_HEADS, MLA_KV_LORA, MLA_V), lambda b: (0, 0, 0))],
        out_specs=pl.BlockSpec((tq, w), lambda b: (b, 0)),
        out_shape=jax.ShapeDtypeStruct((batch * tq, w), BF16),
        compiler_params=_cparams(("parallel",)),
        name="mla_attn_sample",
    )(qn, qr, ckv_new, kr_new, cache_ckv, cache_kr, w_uk_t, w_uv)


def _cross_body(q_ref, k_ref, v_ref, o_ref, *, scale, groups, tq):
    for g in range(groups):
        rows = slice(g * tq, (g + 1) * tq)
        for h in range(MEM_HEADS):
            sl = slice(h * HEAD_DIM, (h + 1) * HEAD_DIM)
            s = _qk(q_ref[rows, sl], k_ref[g, :, sl].astype(BF16)) * scale
            p = jnp.exp(s - jnp.max(s, axis=1, keepdims=True))
            l = jnp.sum(p, axis=1, keepdims=True)
            o = jnp.dot(p.astype(BF16), v_ref[g, :, sl].astype(BF16), preferred_element_type=F32)
            o_ref[rows, sl] = (o / l).astype(o_ref.dtype)


def _cross_attn(q, mem_k, mem_v, *, row0, n_rows, rows_per_stream, groups, tq):
    wq = MEM_HEADS * HEAD_DIM
    n_mem = mem_k.shape[1]
    step_rows = groups * tq
    steps_per_stream_group = rows_per_stream // tq
    rb0 = row0 // step_rows
    mem_spec = pl.BlockSpec((groups, n_mem, wq), lambda i: (i // steps_per_stream_group, 0, 0))
    return pl.pallas_call(
        functools.partial(_cross_body, scale=HEAD_DIM ** -0.5, groups=groups, tq=tq),
        grid=(n_rows // step_rows,),
        in_specs=[pl.BlockSpec((step_rows, wq), lambda i: (rb0 + i, 0)), mem_spec, mem_spec],
        out_specs=pl.BlockSpec((step_rows, wq), lambda i: (i, 0)),
        out_shape=jax.ShapeDtypeStruct((n_rows, wq), BF16),
        compiler_params=_cparams(("parallel",)),
        name="cross_attn",
    )(q, mem_k, mem_v)


def _split_bf16(x):
    hi = x.astype(BF16)
    return hi, (x - hi.astype(F32)).astype(BF16)


def _router_body(x_ref, g_ref, w_ref, h_ref, idx_ref, gate_ref):
    h = _rms(x_ref[...], g_ref[...])
    h_ref[...] = h
    h_hi, h_lo = _split_bf16(h)
    w_hi, w_lo = _split_bf16(w_ref[...])
    logits = (jnp.dot(h_hi, w_hi, preferred_element_type=F32)
              + jnp.dot(h_hi, w_lo, preferred_element_type=F32)
              + jnp.dot(h_lo, w_hi, preferred_element_type=F32))
    lane = lax.broadcasted_iota(jnp.int32, logits.shape, 1)
    logits = jnp.where(lane < N_EXPERTS, logits, NEG_BIG)
    v1 = jnp.max(logits, axis=1, keepdims=True)
    i1 = jnp.min(jnp.where(logits == v1, lane, LANES), axis=1, keepdims=True)
    rest = jnp.where(lane == i1, NEG_BIG, logits)
    v2 = jnp.max(rest, axis=1, keepdims=True)
    i2 = jnp.min(jnp.where(rest == v2, lane, LANES), axis=1, keepdims=True)
    e2 = jnp.exp(v2 - v1)
    g1 = 1.0 / (1.0 + e2)
    g2 = e2 / (1.0 + e2)
    idx_ref[...] = jnp.where(lane == 0, i1, jnp.where(lane == 1, i2, 0))
    gate_ref[...] = jnp.where(lane == 0, g1, jnp.where(lane == 1, g2, 0.0))


def _router(x, gain, w_router, *, tm=512):
    n, d = x.shape
    assert n % tm == 0
    return pl.pallas_call(
        _router_body,
        grid=(n // tm,),
        in_specs=[pl.BlockSpec((tm, d), lambda i: (i, 0)),
                  pl.BlockSpec((1, d), lambda i: (0, 0)),
                  pl.BlockSpec((d, LANES), lambda i: (0, 0))],
        out_specs=[pl.BlockSpec((tm, d), lambda i: (i, 0)),
                   pl.BlockSpec((tm, LANES), lambda i: (i, 0)),
                   pl.BlockSpec((tm, LANES), lambda i: (i, 0))],
        out_shape=[jax.ShapeDtypeStruct((n, d), F32),
                   jax.ShapeDtypeStruct((n, LANES), jnp.int32),
                   jax.ShapeDtypeStruct((n, LANES), F32)],
        compiler_params=_cparams(("parallel",)),
        name="router",
    )(x, gain.reshape(1, d).astype(F32), _pad_lanes(w_router))


def _gather_body(idx_ref, src_ref, out_ref, buf_ref, sem, *, rows):
    base = pl.program_id(0) * rows

    def issue(r, carry):
        pltpu.make_async_copy(src_ref.at[pl.ds(idx_ref[base + r], 1)], buf_ref.at[pl.ds(r, 1)], sem).start()
        return carry

    lax.fori_loop(0, rows, issue, 0)
    pltpu.make_async_copy(src_ref.at[pl.ds(0, rows)], buf_ref, sem).wait()
    out_ref[...] = buf_ref[...].astype(out_ref.dtype)


def _gather_rows(src, row_idx, *, out_dtype, rows=256):
    r, d = row_idx.shape[0], src.shape[1]
    assert r % rows == 0 and src.dtype == F32
    return pl.pallas_call(
        functools.partial(_gather_body, rows=rows),
        grid_spec=pltpu.PrefetchScalarGridSpec(
            num_scalar_prefetch=1, grid=(r // rows,),
            in_specs=[pl.BlockSpec(memory_space=pl.ANY)],
            out_specs=pl.BlockSpec((rows, d), lambda i, idx: (i, 0)),
            scratch_shapes=[pltpu.VMEM((rows, d), F32), pltpu.SemaphoreType.DMA(())]),
        out_shape=jax.ShapeDtypeStruct((r, d), out_dtype),
        compiler_params=_cparams(("arbitrary",)),
        name="gather_rows",
    )(row_idx, src)


def _moe_body(wt_ref, we_ref, ws_ref, wc_ref, wf_ref, x_ref, wg_ref, wu_ref, wd_ref, out_ref, *, ts, nsub):
    w, f = pl.program_id(0), pl.program_id(1)

    @pl.when((f == 0) & (wf_ref[w] == 1))
    def _():
        out_ref[...] = jnp.zeros(out_ref.shape, F32)

    s0, cnt = ws_ref[w], wc_ref[w]
    wg, wu, wd = wg_ref[0].astype(BF16), wu_ref[0].astype(BF16), wd_ref[0].astype(BF16)
    for s in range(nsub):
        @pl.when((s >= s0) & (s < s0 + cnt))
        def _():
            rows = slice(s * ts, (s + 1) * ts)
            out_ref[rows, :] += _swiglu_partial(x_ref[rows, :], wg, wu, wd)


def _moe_ffn(xs, w_gu, w_d, items, *, tm, ts, tf=256):
    r, d = xs.shape
    f = w_d.shape[1]
    nf = f // tf
    n_items = items[0].shape[0]
    assert r % tm == 0 and tm % ts == 0 and f % tf == 0

    def fidx(w, j, wc):
        return jnp.where(wc[w] > 0, j, nf - 1)

    return pl.pallas_call(
        functools.partial(_moe_body, ts=ts, nsub=tm // ts),
        grid_spec=pltpu.PrefetchScalarGridSpec(
            num_scalar_prefetch=5, grid=(n_items, nf),
            in_specs=[pl.BlockSpec((tm, d), lambda w, j, wt, we, ws, wc, wf: (wt[w], 0)),
                      pl.BlockSpec((1, d, tf), lambda w, j, wt, we, ws, wc, wf: (we[w], 0, fidx(w, j, wc))),
                      pl.BlockSpec((1, d, tf), lambda w, j, wt, we, ws, wc, wf: (we[w], 0, nf + fidx(w, j, wc))),
                      pl.BlockSpec((1, tf, d), lambda w, j, wt, we, ws, wc, wf: (we[w], fidx(w, j, wc), 0))],
            out_specs=pl.BlockSpec((tm, d), lambda w, j, wt, we, ws, wc, wf: (wt[w], 0))),
        out_shape=jax.ShapeDtypeStruct((r, d), F32),
        compiler_params=_cparams(("arbitrary", "arbitrary")),
        name="moe_ffn",
    )(*items, xs, w_gu, w_gu, w_d)


def _combine_body(p0_ref, p1_ref, x_ref, gate_ref, ys_ref, out_ref, buf0_ref, buf1_ref, sem, *, rows):
    base = pl.program_id(0) * rows

    def issue(r, carry):
        pltpu.make_async_copy(ys_ref.at[pl.ds(p0_ref[base + r], 1)], buf0_ref.at[pl.ds(r, 1)], sem).start()
        pltpu.make_async_copy(ys_ref.at[pl.ds(p1_ref[base + r], 1)], buf1_ref.at[pl.ds(r, 1)], sem).start()
        return carry

    lax.fori_loop(0, rows, issue, 0)
    pltpu.make_async_copy(ys_ref.at[pl.ds(0, rows)], buf0_ref, sem).wait()
    pltpu.make_async_copy(ys_ref.at[pl.ds(0, rows)], buf1_ref, sem).wait()
    g = gate_ref[...]
    out_ref[...] = x_ref[...] + (g[:, 0:1] * buf0_ref[...] + g[:, 1:2] * buf1_ref[...])


def _moe_combine(x, gates, ys, pos0, pos1, *, rows=256):
    n, d = x.shape
    assert n % rows == 0
    return pl.pallas_call(
        functools.partial(_combine_body, rows=rows),
        grid_spec=pltpu.PrefetchScalarGridSpec(
            num_scalar_prefetch=2, grid=(n // rows,),
            in_specs=[pl.BlockSpec((rows, d), lambda i, p0, p1: (i, 0)),
                      pl.BlockSpec((rows, LANES), lambda i, p0, p1: (i, 0)),
                      pl.BlockSpec(memory_space=pl.ANY)],
            out_specs=pl.BlockSpec((rows, d), lambda i, p0, p1: (i, 0)),
            scratch_shapes=[pltpu.VMEM((rows, d), F32), pltpu.VMEM((rows, d), F32),
                            pltpu.SemaphoreType.DMA(())]),
        out_shape=jax.ShapeDtypeStruct((n, d), F32),
        compiler_params=_cparams(("arbitrary",)),
        name="moe_combine",
    )(pos0, pos1, x, gates, ys)


def _moe_plan(idx, *, tm, ts):
    n = idx.shape[0]
    nsub = tm // ts
    cap = -(-(2 * n + N_EXPERTS * ts) // tm) * tm
    n_tiles, n_subs = cap // tm, cap // ts
    n_items = n_tiles + N_EXPERTS - 1
    e_flat = idx.reshape(-1)
    onehot = (e_flat[:, None] == jnp.arange(N_EXPERTS, dtype=jnp.int32)[None, :]).astype(jnp.int32)
    before = jnp.cumsum(onehot, axis=0) - onehot
    rank = jnp.sum(before * onehot, axis=1)
    counts = jnp.sum(onehot, axis=0)
    padded = ((counts + ts - 1) // ts) * ts
    g_end = jnp.cumsum(padded)
    g_start = g_end - padded
    pos = g_start[e_flat] + rank
    row_token = jnp.zeros((cap,), jnp.int32).at[pos].set(jnp.arange(2 * n, dtype=jnp.int32) // 2)
    s = jnp.arange(n_subs, dtype=jnp.int32)
    sub_e = jnp.sum((s[:, None] * ts >= g_end[None, :]).astype(jnp.int32), axis=1)
    valid = sub_e < N_EXPERTS
    prev_e = jnp.concatenate([jnp.full((1,), -1, jnp.int32), sub_e[:-1]])
    tile_start = (s % nsub) == 0
    new = tile_start | (valid & (sub_e != prev_e))
    item_of_sub = jnp.cumsum(new.astype(jnp.int32)) - 1
    n_used = item_of_sub[-1] + 1
    tgt = jnp.where(new, item_of_sub, n_items)
    w = jnp.arange(n_items, dtype=jnp.int32)
    used = w < n_used
    it_tile = jnp.full((n_items,), n_tiles - 1, jnp.int32).at[tgt].set(s // nsub, mode="drop")
    it_exp = jnp.full((n_items,), N_EXPERTS - 1, jnp.int32).at[tgt].set(
        jnp.minimum(sub_e, N_EXPERTS - 1), mode="drop")
    it_s0 = jnp.zeros((n_items,), jnp.int32).at[tgt].set(s % nsub, mode="drop")
    it_first = jnp.zeros((n_items,), jnp.int32).at[tgt].set(tile_start.astype(jnp.int32), mode="drop")
    it_cnt = jnp.zeros((n_items,), jnp.int32).at[item_of_sub].add(valid.astype(jnp.int32), mode="drop")
    it_cnt = jnp.where(used, it_cnt, 0)
    return pos[0::2], pos[1::2], row_token, (it_tile, it_exp, it_s0, it_cnt, it_first)


def _moe_layer(x, gain, w_router, w_gu, w_d, *, tm=1024, ts=256):
    h, idx, gates = _router(x, gain, w_router)
    pos0, pos1, row_token, items = _moe_plan(idx[:, :2], tm=tm, ts=ts)
    xs = _gather_rows(h, row_token, out_dtype=BF16)
    ys = _moe_ffn(xs, w_gu, w_d, items, tm=tm, ts=ts)
    return _moe_combine(x, gates, ys, pos0, pos1)


def _fox_layer(x, gain, w_in, b_f, w_o, cache_k, cache_v, cache_logf, layer, *, bp, tp, bs, tq):
    n_p = bp * tp
    w = FOX_HEADS * HEAD_DIM
    qkv, qkv_b = _matmul(x, w_in, gain=gain, m=3 * w, out_dtypes=(F32, BF16))
    logf = _matmul(x, _pad_lanes(w_in[:, 3 * w:]), gain=gain, bias=_pad_lanes(b_f),
                   act="log_sigmoid")[:, :FOX_HEADS]
    logf_p = logf[:n_p].reshape(bp, tp, FOX_HEADS)
    logf_s = logf[n_p:].reshape(bs, tq, FOX_HEADS)
    past = cache_logf.shape[2]
    cs_p = _cumsum_time(logf_p)
    cs_s = _cumsum_time(jnp.concatenate([cache_logf[layer], logf_s], axis=1))
    o_p = _fox_attn_prompt(qkv_b, cs_p, cs_p.transpose(0, 2, 1), batch=bp, seq=tp)
    o_s = _fox_attn_sample(qkv_b, cache_k, cache_v, layer, cs_s[:, past:], cs_s.transpose(0, 2, 1),
                           row0=n_p, batch=bs, tq=tq)
    x = _matmul(jnp.concatenate([o_p, o_s], axis=0), w_o, res=x)
    return x, qkv[:, w:2 * w], qkv[:, 2 * w:], logf


def _mla_layer(x, gain, w_in, g_q, g_kv, w_uq, w_ukv, w_o, cache_ckv, cache_kr, layer, cos, sin,
               *, bp, tp, bs, tq):
    n_p = bp * tp
    h, c, r = MLA_HEADS, MLA_KV_LORA, MLA_ROPE
    w_kr = w_in[:, MLA_Q_LORA + c:]
    w_in_big = jnp.concatenate([w_in[:, :MLA_Q_LORA + c], _pad_lanes(w_kr), _pad_lanes(_swap_halves(w_kr))], axis=1)
    w_q = w_uq.reshape(MLA_Q_LORA, h, MLA_NOPE + r)
    w_qr = w_q[:, :, MLA_NOPE:]
    w_uq_big = jnp.concatenate([w_q[:, :, :MLA_NOPE].reshape(MLA_Q_LORA, h * MLA_NOPE),
                                _pad_lanes(w_qr).reshape(MLA_Q_LORA, h * LANES),
                                _pad_lanes(_swap_halves(w_qr)).reshape(MLA_Q_LORA, h * LANES)], axis=1)
    w_kv = w_ukv.reshape(c, h, MLA_NOPE + MLA_V)
    w_ukv_split = jnp.concatenate([w_kv[:, :, :MLA_NOPE].reshape(c, h * MLA_NOPE),
                                   w_kv[:, :, MLA_NOPE:].reshape(c, h * MLA_V)], axis=1)
    w_uk_t = w_kv[:, :, :MLA_NOPE].transpose(1, 2, 0).astype(BF16)
    w_uv = w_kv[:, :, MLA_NOPE:].transpose(1, 0, 2).astype(BF16)

    a = _matmul(x, w_in_big, gain=gain, tn=256)
    ckv, ckv_b = _rmsnorm(a, g_kv, x_col=1, k=c, out_dtypes=(F32, BF16))
    kr, kr_b = _rope(a, cos, sin, x_col=(MLA_Q_LORA + c) // (2 * LANES), width=LANES, out_dtypes=(F32, BF16))
    qn = _matmul(a, w_uq_big, gain=g_q, x_col=0, k=MLA_Q_LORA, m=h * MLA_NOPE, out_dtypes=(BF16,))
    qr2 = _matmul(a, w_uq_big, gain=g_q, x_col=0, k=MLA_Q_LORA, w_col0=h * MLA_NOPE)
    qr = _rope(qr2, cos, sin, width=h * LANES, out_dtypes=(BF16,))
    kv = _matmul(ckv_b, w_ukv_split, n_rows=n_p, out_dtypes=(BF16,))
    o_p = _mla_attn_prompt(qn, qr, kv, kr_b, batch=bp, seq=tp)
    o_s = _mla_attn_sample(qn, qr, ckv_b, kr_b, cache_ckv, cache_kr, layer, w_uk_t, w_uv,
                           row0=n_p, batch=bs, tq=tq)
    x = _matmul(jnp.concatenate([o_p, o_s], axis=0), w_o, res=x)
    return x, ckv, kr[:, :r]


def _cross_layer(x, gain, w_q, w_o, mem_p, mem_s, *, bp, tp, bs, tq):
    n_p = bp * tp
    q = _matmul(x, w_q, gain=gain, out_dtypes=(BF16,))
    o_p = _cross_attn(q, mem_p[0], mem_p[1], row0=0, n_rows=n_p, rows_per_stream=tp, groups=1, tq=1024)
    o_s = _cross_attn(q, mem_s[0], mem_s[1], row0=n_p, n_rows=bs * tq, rows_per_stream=tq, groups=8, tq=tq)
    return _matmul(jnp.concatenate([o_p, o_s], axis=0), w_o, res=x)


def kernel(x_prompt, x_sample, cache_fox_k, cache_fox_v, cache_fox_logf, cache_mla_ckv, cache_mla_krope, cache_mem_k, cache_mem_v, mem_prompt, norm_mix, norm_cross, norm_ffn, norm_final, fox_w_in, fox_b_f, fox_w_o, mla_w_in, mla_g_q, mla_g_kv, mla_w_uq, mla_w_ukv, mla_w_o, cross_w_q, cross_w_kv, cross_w_o, ffn_w_gu, ffn_w_d, moe_w_router, moe_w_gu, moe_w_d):
    bp, tp, d = x_prompt.shape
    bs, tq, _ = x_sample.shape
    n_p = bp * tp
    depth = norm_mix.shape[0]
    past = cache_fox_k.shape[2]
    n_mem = mem_prompt.shape[1]
    mem_w = MEM_HEADS * HEAD_DIM
    dims = dict(bp=bp, tp=tp, bs=bs, tq=tq)

    x = jnp.concatenate([x_prompt.reshape(n_p, d), x_sample.reshape(bs * tq, d)], axis=0)
    pos = np.concatenate([np.tile(np.arange(tp), bp), np.tile(past + np.arange(tq), bs)])
    cos, sin = _rope_tables(pos)
    mem_flat = mem_prompt.reshape(bp * n_mem, d)

    fox_rows, mla_rows, mem_rows = [], [], []
    for i in range(depth):
        j = i // 2
        if i % 2 == 0:
            x, k, v, lf = _fox_layer(x, norm_mix[i], fox_w_in[j], fox_b_f[j], fox_w_o[j],
                                     cache_fox_k, cache_fox_v, cache_fox_logf, j, **dims)
            fox_rows.append((k, v, lf))
        else:
            x, ckv, kr = _mla_layer(x, norm_mix[i], mla_w_in[j], mla_g_q[j], mla_g_kv[j], mla_w_uq[j],
                                    mla_w_ukv[j], mla_w_o[j], cache_mla_ckv, cache_mla_krope, j,
                                    cos, sin, **dims)
            mla_rows.append((ckv, kr))
        kv_p = _matmul(mem_flat, cross_w_kv[i])
        mem_p = (kv_p[:, :mem_w].reshape(bp, n_mem, mem_w), kv_p[:, mem_w:].reshape(bp, n_mem, mem_w))
        mem_rows.append(mem_p)
        mem_s = (cache_mem_k[i].reshape(bs, n_mem, mem_w), cache_mem_v[i].reshape(bs, n_mem, mem_w))
        x = _cross_layer(x, norm_cross[i], cross_w_q[i], cross_w_o[i], mem_p, mem_s, **dims)
        if i % 2 == 0:
            x = _ffn(x, norm_ffn[i], ffn_w_gu[j], ffn_w_d[j])
        else:
            x = _moe_layer(x, norm_ffn[i], moe_w_router[j], moe_w_gu[j], moe_w_d[j])
    y = _rmsnorm(x, norm_final)

    def split(rows, tail):
        return rows[:n_p].reshape((bp, tp) + tail), rows[n_p:].reshape((bs, tq) + tail)

    def stacked(rows_list, tail):
        parts = [split(r, tail) for r in rows_list]
        return jnp.stack([p[0] for p in parts]), jnp.stack([p[1] for p in parts])

    y_p, y_s = split(y, (d,))
    hd = (FOX_HEADS, HEAD_DIM)
    fox_k_p, fox_k_s = stacked([r[0] for r in fox_rows], hd)
    fox_v_p, fox_v_s = stacked([r[1] for r in fox_rows], hd)
    fox_lf_p, fox_lf_s = stacked([r[2] for r in fox_rows], (FOX_HEADS,))
    mla_ckv_p, mla_ckv_s = stacked([r[0] for r in mla_rows], (MLA_KV_LORA,))
    mla_kr_p, mla_kr_s = stacked([r[1] for r in mla_rows], (MLA_ROPE,))
    mem_k_p = jnp.stack([m[0].reshape(bp, n_mem, MEM_HEADS, HEAD_DIM) for m in mem_rows])
    mem_v_p = jnp.stack([m[1].reshape(bp, n_mem, MEM_HEADS, HEAD_DIM) for m in mem_rows])
    return (y_p, y_s, fox_k_p, fox_v_p, fox_lf_p, mla_ckv_p, mla_kr_p, mem_k_p, mem_v_p,
            fox_k_s, fox_v_s, fox_lf_s, mla_ckv_s, mla_kr_s)
```

```python
import functools

import numpy as np
import jax
import jax.numpy as jnp
from jax import lax
from jax.experimental import pallas as pl
from jax.experimental.pallas import tpu as pltpu

F32 = jnp.float32
BF16 = jnp.bfloat16

D_MODEL = 2048
CHUNK = 64
RMS_EPS = 1e-6
FOX_HEADS = 16
HEAD_DIM = 128
MLA_HEADS = 16
MLA_Q_LORA = 512
MLA_KV_LORA = 512
MLA_NOPE = 128
MLA_ROPE = 64
MLA_V = 128
ROPE_BASE = 10000.0
MEM_HEADS = 4
N_EXPERTS = 8
LANES = 128

VMEM_LIMIT_BYTES = 56 * 1024 * 1024
NEG_BIG = -1e30
LOG2E = 1.4426950408889634


def _cparams(sem):
    return pltpu.CompilerParams(dimension_semantics=sem, vmem_limit_bytes=VMEM_LIMIT_BYTES)


def _rms(xf, g):
    return xf * lax.rsqrt(jnp.mean(xf * xf, axis=-1, keepdims=True) + RMS_EPS) * g


def _log_sigmoid(x):
    return jnp.minimum(x, 0.0) - jnp.log1p(jnp.exp(-jnp.abs(x)))


def _split3_bf16(x):
    hi = x.astype(BF16)
    r1 = x - hi.astype(F32)
    mid = r1.astype(BF16)
    lo = (r1 - mid.astype(F32)).astype(BF16)
    return hi, mid, lo


def _mm_body(*refs, has_norm, has_bias, has_scale, has_res, act, use_scratch, n_out):
    it = iter(refs)
    x_ref = next(it)
    g_ref = next(it) if has_norm else None
    w_ref = next(it)
    b_ref = next(it) if has_bias else None
    s_ref = next(it) if has_scale else None
    r_ref = next(it) if has_res else None
    out_refs = [next(it) for _ in range(n_out)]
    xb_ref = next(it) if use_scratch else None

    if use_scratch:
        @pl.when(pl.program_id(1) == 0)
        def _():
            xf = x_ref[...].astype(F32)
            if has_norm:
                xf = _rms(xf, g_ref[...])
            xb_ref[...] = xf.astype(BF16)
        xb = xb_ref[...]
    else:
        xb = x_ref[...]
    acc = jnp.dot(xb, w_ref[...].astype(BF16), preferred_element_type=F32)
    if has_bias:
        acc = acc + b_ref[...]
    if act == "log_sigmoid":
        acc = _log_sigmoid(acc)
    if has_scale:
        acc = acc * s_ref[...]
    if has_res:
        acc = acc + r_ref[...]
    for o in out_refs:
        o[...] = acc.astype(o.dtype)


def _matmul(x, w, *, layer=None, gain=None, bias=None, col_scale=None, res=None, act=None, x_col=0,
            k=None, w_col0=0, m=None, n_rows=None, out_dtypes=(F32,), tm=1024, tn=512):
    n = x.shape[0] if n_rows is None else n_rows
    k = x.shape[1] if k is None else k
    m = w.shape[-1] - w_col0 if m is None else m
    tm = min(tm, n)
    tn = min(tn, m)
    assert n % tm == 0 and m % tn == 0 and w_col0 % tn == 0 and w.shape[-2] == k
    assert (w.ndim == 3) == (layer is not None)
    has_norm, has_bias, has_res = gain is not None, bias is not None, res is not None
    has_scale = col_scale is not None
    use_scratch = has_norm or x.dtype != BF16
    col0 = w_col0 // tn
    row_vec = pl.BlockSpec((1, tn), lambda i, j: (0, j))
    in_specs = [pl.BlockSpec((tm, k), lambda i, j: (i, x_col))]
    args = [x]
    if has_norm:
        in_specs.append(pl.BlockSpec((1, k), lambda i, j: (0, 0)))
        args.append(gain.reshape(1, k).astype(F32))
    if layer is None:
        in_specs.append(pl.BlockSpec((k, tn), lambda i, j: (0, j + col0)))
    else:
        in_specs.append(pl.BlockSpec((None, k, tn), lambda i, j: (layer, 0, j + col0)))
    args.append(w)
    if has_bias:
        in_specs.append(row_vec)
        args.append(bias.reshape(1, m).astype(F32))
    if has_scale:
        in_specs.append(row_vec)
        args.append(col_scale.reshape(1, m).astype(F32))
    if has_res:
        in_specs.append(pl.BlockSpec((tm, tn), lambda i, j: (i, j)))
        args.append(res)
    body = functools.partial(_mm_body, has_norm=has_norm, has_bias=has_bias, has_scale=has_scale,
                             has_res=has_res, act=act, use_scratch=use_scratch, n_out=len(out_dtypes))
    outs = pl.pallas_call(
        body,
        grid=(n // tm, m // tn),
        in_specs=in_specs,
        out_specs=[pl.BlockSpec((tm, tn), lambda i, j: (i, j)) for _ in out_dtypes],
        out_shape=[jax.ShapeDtypeStruct((n, m), dt) for dt in out_dtypes],
        scratch_shapes=[pltpu.VMEM((tm, k), BF16)] if use_scratch else [],
        compiler_params=_cparams(("parallel", "arbitrary")),
        name="matmul",
    )(*args)
    return outs[0] if len(out_dtypes) == 1 else outs


def _norm_body(x_ref, g_ref, *out_refs):
    y = _rms(x_ref[...].astype(F32), g_ref[...])
    for o in out_refs:
        o[...] = y.astype(o.dtype)


def _rmsnorm(x, gain, *, x_col=0, k=None, out_dtypes=(F32,), tm=1024):
    n = x.shape[0]
    k = x.shape[1] if k is None else k
    tm = min(tm, n)
    assert n % tm == 0
    outs = pl.pallas_call(
        _norm_body,
        grid=(n // tm,),
        in_specs=[pl.BlockSpec((tm, k), lambda i: (i, x_col)),
                  pl.BlockSpec((1, k), lambda i: (0, 0))],
        out_specs=[pl.BlockSpec((tm, k), lambda i: (i, 0)) for _ in out_dtypes],
        out_shape=[jax.ShapeDtypeStruct((n, k), dt) for dt in out_dtypes],
        compiler_params=_cparams(("parallel",)),
        name="rmsnorm",
    )(x, gain.reshape(1, k).astype(F32))
    return outs[0] if len(out_dtypes) == 1 else outs


def _swiglu_partial(xb, wg, wu, wd):
    g = jnp.dot(xb, wg.astype(BF16), preferred_element_type=F32)
    u = jnp.dot(xb, wu.astype(BF16), preferred_element_type=F32)
    a = (g * jax.nn.sigmoid(g)) * u
    return jnp.dot(a.astype(BF16), wd.astype(BF16), preferred_element_type=F32)


def _ffn_body(x_ref, g_ref, wg_ref, wu_ref, wd_ref, out_ref, xb_ref):
    @pl.when(pl.program_id(1) == 0)
    def _():
        xf = x_ref[...]
        xb_ref[...] = _rms(xf, g_ref[...]).astype(BF16)
        out_ref[...] = xf
    out_ref[...] += _swiglu_partial(xb_ref[...], wg_ref[...], wu_ref[...], wd_ref[...])


def _ffn(x, gain, w_gu, w_d, layer, *, tm=1024, tf=256):
    n, d = x.shape
    f = w_d.shape[1]
    assert n % tm == 0 and f % tf == 0
    nf = f // tf
    return pl.pallas_call(
        _ffn_body,
        grid=(n // tm, nf),
        in_specs=[pl.BlockSpec((tm, d), lambda i, j: (i, 0)),
                  pl.BlockSpec((1, d), lambda i, j: (0, 0)),
                  pl.BlockSpec((None, d, tf), lambda i, j: (layer, 0, j)),
                  pl.BlockSpec((None, d, tf), lambda i, j: (layer, 0, j + nf)),
                  pl.BlockSpec((None, tf, d), lambda i, j: (layer, j, 0))],
        out_specs=pl.BlockSpec((tm, d), lambda i, j: (i, 0)),
        out_shape=jax.ShapeDtypeStruct((n, d), F32),
        scratch_shapes=[pltpu.VMEM((tm, d), BF16)],
        compiler_params=_cparams(("parallel", "arbitrary")),
        name="ffn",
    )(x, gain.reshape(1, d).astype(F32), w_gu, w_gu, w_d)


def _cumsum_body(x_ref, out_ref, *, chunk):
    t = x_ref.shape[1]
    row = lax.broadcasted_iota(jnp.int32, (chunk, chunk), 0)
    col = lax.broadcasted_iota(jnp.int32, (chunk, chunk), 1)
    tri = jnp.where(col <= row, 1.0, 0.0).astype(BF16)
    carry = jnp.zeros((1, x_ref.shape[2]), F32)
    for c in range(t // chunk):
        hi, mid, lo = _split3_bf16(x_ref[0, c * chunk:(c + 1) * chunk, :])
        cs = (jnp.dot(tri, hi, preferred_element_type=F32)
              + jnp.dot(tri, mid, preferred_element_type=F32)
              + jnp.dot(tri, lo, preferred_element_type=F32)) + carry
        out_ref[0, c * chunk:(c + 1) * chunk, :] = cs
        carry = cs[chunk - 1:chunk, :]


def _cumsum_time(x):
    b, t, h = x.shape
    chunk = next(c for c in (256, 128, 64, 32, 16, 8) if t % c == 0)
    return pl.pallas_call(
        functools.partial(_cumsum_body, chunk=chunk),
        grid=(b,),
        in_specs=[pl.BlockSpec((1, t, h), lambda i: (i, 0, 0))],
        out_specs=pl.BlockSpec((1, t, h), lambda i: (i, 0, 0)),
        out_shape=jax.ShapeDtypeStruct((b, t, h), F32),
        compiler_params=_cparams(("parallel",)),
        name="cumsum",
    )(x)


def _qk(q, k):
    return lax.dot_general(q, k, (((1,), (1,)), ((), ())), preferred_element_type=F32)


def _flash_step(s, v, m_ref, l_ref, acc_ref, h, lanes):
    m_old = m_ref[h]
    m_new = jnp.maximum(m_old, jnp.max(s, axis=1, keepdims=True))
    alpha = jnp.exp2(m_old - m_new)
    p = jnp.exp2(s - jnp.concatenate([m_new] * (s.shape[1] // LANES), axis=1))
    l_ref[h] = alpha * l_ref[h] + jnp.sum(p, axis=1, keepdims=True)
    acc_ref[:, lanes] = alpha * acc_ref[:, lanes] + jnp.dot(p.astype(BF16), v, preferred_element_type=F32)
    m_ref[h] = m_new


def _flash_init(m_ref, l_ref, acc_ref):
    m_ref[...] = jnp.full(m_ref.shape, NEG_BIG, F32)
    l_ref[...] = jnp.zeros(l_ref.shape, F32)
    acc_ref[...] = jnp.zeros(acc_ref.shape, F32)


def _flash_finish(o_ref, l_ref, acc_ref, heads):
    for h in range(heads):
        sl = slice(h * LANES, (h + 1) * LANES)
        o_ref[:, sl] = (acc_ref[:, sl] / l_ref[h]).astype(o_ref.dtype)


def _flash_scratch(t, heads):
    return [pltpu.VMEM((heads, t, LANES), F32), pltpu.VMEM((heads, t, LANES), F32),
            pltpu.VMEM((t, heads * LANES), F32)]


def _fox_aug_body(cs_ref, aq_ref, ak_ref):
    c = cs_ref[...] * LOG2E
    shape = (c.shape[0], LANES)
    lane = lax.broadcasted_iota(jnp.int32, shape, 1)
    for h in range(FOX_HEADS):
        hi, mid, lo = (p.astype(F32) for p in _split3_bf16(jnp.broadcast_to(c[:, h:h + 1], shape)))
        aq = jnp.where(lane == 0, hi, jnp.where(lane == 1, mid, jnp.where(lane == 2, lo,
                       jnp.where(lane < 6, 1.0, 0.0))))
        ak = jnp.where(lane < 3, 1.0, jnp.where(lane == 3, -hi, jnp.where(lane == 4, -mid,
                       jnp.where(lane == 5, -lo, 0.0))))
        sl = slice(h * LANES, (h + 1) * LANES)
        aq_ref[:, sl] = aq.astype(BF16)
        ak_ref[:, sl] = ak.astype(BF16)


def _fox_aug(csum_rows, *, tm=512):
    n, h = csum_rows.shape
    assert n % tm == 0
    return pl.pallas_call(
        _fox_aug_body,
        grid=(n // tm,),
        in_specs=[pl.BlockSpec((tm, h), lambda i: (i, 0))],
        out_specs=[pl.BlockSpec((tm, h * LANES), lambda i: (i, 0))] * 2,
        out_shape=[jax.ShapeDtypeStruct((n, h * LANES), BF16)] * 2,
        compiler_params=_cparams(("parallel",)),
        name="fox_aug",
    )(csum_rows)


def _fox_prompt_body(q_ref, aq_ref, k_ref, v_ref, ak_ref, o_ref, m_ref, l_ref, acc_ref, *, t):
    qi, ki = pl.program_id(1), pl.program_id(2)

    @pl.when(ki == 0)
    def _():
        _flash_init(m_ref, l_ref, acc_ref)

    def block(diagonal):
        if diagonal:
            allowed = (lax.broadcasted_iota(jnp.int32, (t, t), 1)
                       <= lax.broadcasted_iota(jnp.int32, (t, t), 0))
        for h in range(FOX_HEADS):
            sl = slice(h * LANES, (h + 1) * LANES)
            q = jnp.concatenate([q_ref[:, sl], aq_ref[:, sl]], axis=1)
            k = jnp.concatenate([k_ref[:, sl], ak_ref[:, sl]], axis=1)
            s = _qk(q, k)
            if diagonal:
                s = jnp.where(allowed, s, NEG_BIG)
            _flash_step(s, v_ref[:, sl], m_ref, l_ref, acc_ref, h, sl)

    @pl.when(ki < qi)
    def _():
        block(False)

    @pl.when(ki == qi)
    def _():
        block(True)
        _flash_finish(o_ref, l_ref, acc_ref, FOX_HEADS)


def _fox_attn_prompt(q, kv, aq, ak, *, batch, seq, t=512):
    w = FOX_HEADS * HEAD_DIM
    nt = seq // t
    assert seq % t == 0
    q_row = lambda b, i, j: b * nt + i
    k_row = lambda b, i, j: b * nt + jnp.minimum(i, j)
    return pl.pallas_call(
        functools.partial(_fox_prompt_body, t=t),
        grid=(batch, nt, nt),
        in_specs=[pl.BlockSpec((t, w), lambda b, i, j: (q_row(b, i, j), 0)),
                  pl.BlockSpec((t, w), lambda b, i, j: (q_row(b, i, j), 0)),
                  pl.BlockSpec((t, w), lambda b, i, j: (k_row(b, i, j), 0)),
                  pl.BlockSpec((t, w), lambda b, i, j: (k_row(b, i, j), 1)),
                  pl.BlockSpec((t, w), lambda b, i, j: (k_row(b, i, j), 0))],
        out_specs=pl.BlockSpec((t, w), lambda b, i, j: (q_row(b, i, j), 0)),
        out_shape=jax.ShapeDtypeStruct((batch * seq, w), BF16),
        scratch_shapes=_flash_scratch(t, FOX_HEADS),
        compiler_params=_cparams(("parallel", "parallel", "arbitrary")),
        name="fox_attn_prompt",
    )(q, aq, kv, kv, ak)


def _fox_sample_body(q_ref, kn_ref, vn_ref, kc_ref, vc_ref, cq_ref, ck_ref, o_ref, *, past, tq):
    allowed = (lax.broadcasted_iota(jnp.int32, (tq, tq), 1)
               <= lax.broadcasted_iota(jnp.int32, (tq, tq), 0))
    for h in range(FOX_HEADS):
        sl = slice(h * HEAD_DIM, (h + 1) * HEAD_DIM)
        q = q_ref[:, sl]
        cq = cq_ref[0, :, h:h + 1] * LOG2E
        ck = ck_ref[0, h:h + 1, :] * LOG2E
        kc = kc_ref[0, 0, pl.ds(h, past, stride=FOX_HEADS), :].astype(BF16)
        vc = vc_ref[0, 0, pl.ds(h, past, stride=FOX_HEADS), :].astype(BF16)
        s_c = _qk(q, kc) + (cq - ck[:, :past])
        s_n = _qk(q, kn_ref[:, sl]) + (cq - ck[:, past:])
        s_n = jnp.where(allowed, s_n, NEG_BIG)
        m = jnp.maximum(jnp.max(s_c, axis=1, keepdims=True), jnp.max(s_n, axis=1, keepdims=True))
        p_c = jnp.exp2(s_c - m)
        p_n = jnp.exp2(s_n - m)
        l = jnp.sum(p_c, axis=1, keepdims=True) + jnp.sum(p_n, axis=1, keepdims=True)
        o = (jnp.dot(p_c.astype(BF16), vc, preferred_element_type=F32)
             + jnp.dot(p_n.astype(BF16), vn_ref[:, sl], preferred_element_type=F32))
        o_ref[:, sl] = (o / l).astype(o_ref.dtype)


def _fox_attn_sample(q, kv, cache_k, cache_v, layer, csum_q, csum_t, *, row0, batch, tq):
    w = FOX_HEADS * HEAD_DIM
    n_layers, n_streams, past = cache_k.shape[:3]
    rb0 = row0 // tq
    flat = (n_layers, n_streams, past * FOX_HEADS, HEAD_DIM)
    cache_spec = pl.BlockSpec((1, 1, past * FOX_HEADS, HEAD_DIM), lambda b: (layer, b, 0, 0))
    return pl.pallas_call(
        functools.partial(_fox_sample_body, past=past, tq=tq),
        grid=(batch,),
        in_specs=[pl.BlockSpec((tq, w), lambda b: (rb0 + b, 0)),
                  pl.BlockSpec((tq, w), lambda b: (rb0 + b, 0)),
                  pl.BlockSpec((tq, w), lambda b: (rb0 + b, 1)),
                  cache_spec, cache_spec,
                  pl.BlockSpec((1, tq, FOX_HEADS), lambda b: (b, 0, 0)),
                  pl.BlockSpec((1, FOX_HEADS, past + tq), lambda b: (b, 0, 0))],
        out_specs=pl.BlockSpec((tq, w), lambda b: (b, 0)),
        out_shape=jax.ShapeDtypeStruct((batch * tq, w), BF16),
        compiler_params=_cparams(("parallel",)),
        name="fox_attn_sample",
    )(q, kv, kv, cache_k.reshape(flat), cache_v.reshape(flat), csum_q, csum_t)


def _rope_body(x_ref, cos_ref, sin_ref, *out_refs, width):
    cos, sin = cos_ref[...], sin_ref[...]
    for h in range(width // LANES):
        a = x_ref[:, h * LANES:(h + 1) * LANES]
        b = x_ref[:, width + h * LANES:width + (h + 1) * LANES]
        y = a * cos + b * sin
        for o in out_refs:
            o[:, h * LANES:(h + 1) * LANES] = y.astype(o.dtype)


def _rope(x, cos, sin, *, x_col=0, width, out_dtypes, tm=512):
    n = x.shape[0]
    assert n % tm == 0
    outs = pl.pallas_call(
        functools.partial(_rope_body, width=width),
        grid=(n // tm,),
        in_specs=[pl.BlockSpec((tm, 2 * width), lambda i: (i, x_col)),
                  pl.BlockSpec((tm, LANES), lambda i: (i, 0)),
                  pl.BlockSpec((tm, LANES), lambda i: (i, 0))],
        out_specs=[pl.BlockSpec((tm, width), lambda i: (i, 0)) for _ in out_dtypes],
        out_shape=[jax.ShapeDtypeStruct((n, width), dt) for dt in out_dtypes],
        compiler_params=_cparams(("parallel",)),
        name="rope",
    )(x, cos, sin)
    return outs[0] if len(out_dtypes) == 1 else outs


def _rope_tables(pos):
    half = MLA_ROPE // 2
    inv_freq = ROPE_BASE ** (-np.arange(half, dtype=np.float32) / half)
    ang = jnp.asarray(pos, F32)[:, None] * jnp.asarray(inv_freq)[None, :]
    zeros = jnp.zeros((ang.shape[0], LANES - MLA_ROPE), F32)
    cos = jnp.concatenate([jnp.cos(ang), jnp.cos(ang), zeros], axis=1)
    sin = jnp.concatenate([jnp.sin(ang), jnp.sin(ang), zeros], axis=1)
    return cos, sin


def _swap_halves(w):
    half = w.shape[-1] // 2
    return jnp.concatenate([-w[..., half:], w[..., :half]], axis=-1)


def _pad_lanes(w):
    return jnp.concatenate([w, jnp.zeros(w.shape[:-1] + (LANES - w.shape[-1],), w.dtype)], axis=-1)


def _mla_prompt_body(qn_ref, qr_ref, kn_ref, v_ref, kr_ref, o_ref, m_ref, l_ref, acc_ref, *, t):
    qi, ki = pl.program_id(1), pl.program_id(2)

    @pl.when(ki == 0)
    def _():
        _flash_init(m_ref, l_ref, acc_ref)

    def block(diagonal):
        if diagonal:
            allowed = ((lax.broadcasted_iota(jnp.int32, (t, t), 1) // CHUNK)
                       <= (lax.broadcasted_iota(jnp.int32, (t, t), 0) // CHUNK))
        kr = kr_ref[...]
        for h in range(MLA_HEADS):
            sl = slice(h * LANES, (h + 1) * LANES)
            q = jnp.concatenate([qn_ref[:, sl], qr_ref[:, sl]], axis=1)
            k = jnp.concatenate([kn_ref[:, sl], kr], axis=1)
            s = _qk(q, k)
            if diagonal:
                s = jnp.where(allowed, s, NEG_BIG)
            _flash_step(s, v_ref[:, sl], m_ref, l_ref, acc_ref, h, sl)

    @pl.when(ki < qi)
    def _():
        block(False)

    @pl.when(ki == qi)
    def _():
        block(True)
        _flash_finish(o_ref, l_ref, acc_ref, MLA_HEADS)


def _mla_attn_prompt(qn, qr, kv, kr, *, batch, seq, t=512):
    w = MLA_HEADS * LANES
    nt = seq // t
    assert seq % t == 0 and t % CHUNK == 0
    kv_row = lambda b, i, j: b * nt + jnp.minimum(i, j)
    return pl.pallas_call(
        functools.partial(_mla_prompt_body, t=t),
        grid=(batch, nt, nt),
        in_specs=[pl.BlockSpec((t, w), lambda b, i, j: (b * nt + i, 0)),
                  pl.BlockSpec((t, w), lambda b, i, j: (b * nt + i, 0)),
                  pl.BlockSpec((t, w), lambda b, i, j: (kv_row(b, i, j), 0)),
                  pl.BlockSpec((t, w), lambda b, i, j: (kv_row(b, i, j), 1)),
                  pl.BlockSpec((t, LANES), lambda b, i, j: (kv_row(b, i, j), 0))],
        out_specs=pl.BlockSpec((t, w), lambda b, i, j: (b * nt + i, 0)),
        out_shape=jax.ShapeDtypeStruct((batch * seq, w), BF16),
        scratch_shapes=_flash_scratch(t, MLA_HEADS),
        compiler_params=_cparams(("parallel", "parallel", "arbitrary")),
        name="mla_attn_prompt",
    )(qn, qr, kv, kv, kr)


def _mla_sample_body(qn_ref, qr_ref, cn_ref, krn_ref, cc_ref, krc_ref, wuk_ref, wuv_ref, o_ref,
                     *, past, tq):
    heads = MLA_HEADS
    q_lat = jnp.concatenate(
        [jnp.dot(qn_ref[:, h * LANES:(h + 1) * LANES], wuk_ref[h], preferred_element_type=F32)
         for h in range(heads)], axis=0).astype(BF16)
    q_r = jnp.concatenate([qr_ref[:, h * LANES:(h + 1) * LANES] for h in range(heads)], axis=0)
    c_c = cc_ref[0, 0].astype(BF16)
    kr_c = krc_ref[0, 0].astype(BF16)
    c_n = cn_ref[...]
    s_c = _qk(q_lat, c_c) + _qk(q_r[:, :MLA_ROPE], kr_c)
    s_n = _qk(q_lat, c_n) + _qk(q_r, krn_ref[...])
    vis_c = (np.arange(past)[None, :] // CHUNK) <= ((past + np.arange(tq))[:, None] // CHUNK)
    vis_n = ((past + np.arange(tq))[None, :] // CHUNK) <= ((past + np.arange(tq))[:, None] // CHUNK)
    if not vis_c.all():
        frame = lax.broadcasted_iota(jnp.int32, s_c.shape, 0) % tq
        key = lax.broadcasted_iota(jnp.int32, s_c.shape, 1)
        s_c = jnp.where((key // CHUNK) <= ((past + frame) // CHUNK), s_c, NEG_BIG)
    if not vis_n.all():
        frame = lax.broadcasted_iota(jnp.int32, s_n.shape, 0) % tq
        key = lax.broadcasted_iota(jnp.int32, s_n.shape, 1)
        s_n = jnp.where(((past + key) // CHUNK) <= ((past + frame) // CHUNK), s_n, NEG_BIG)
    m = jnp.maximum(jnp.max(s_c, axis=1, keepdims=True), jnp.max(s_n, axis=1, keepdims=True))
    p_c = jnp.exp2(s_c - m)
    p_n = jnp.exp2(s_n - m)
    l = jnp.sum(p_c, axis=1, keepdims=True) + jnp.sum(p_n, axis=1, keepdims=True)
    o_lat = (jnp.dot(p_c.astype(BF16), c_c, preferred_element_type=F32)
             + jnp.dot(p_n.astype(BF16), c_n, preferred_element_type=F32)) / l
    o_lat = o_lat.astype(BF16)
    for h in range(heads):
        o_ref[:, h * LANES:(h + 1) * LANES] = jnp.dot(
            o_lat[h * tq:(h + 1) * tq], wuv_ref[h], preferred_element_type=F32).astype(o_ref.dtype)


def _mla_attn_sample(qn, qr, ckv_new, kr_new, cache_ckv, cache_kr, layer, w_uk_t, w_uv, *, row0, batch, tq):
    w = MLA_HEADS * LANES
    past = cache_ckv.shape[2]
    rb0 = row0 // tq
    return pl.pallas_call(
        functools.partial(_mla_sample_body, past=past, tq=tq),
        grid=(batch,),
        in_specs=[pl.BlockSpec((tq, w), lambda b: (rb0 + b, 0)),
                  pl.BlockSpec((tq, w), lambda b: (rb0 + b, 0)),
                  pl.BlockSpec((tq, MLA_KV_LORA), lambda b: (rb0 + b, 0)),
                  pl.BlockSpec((tq, LANES), lambda b: (rb0 + b, 0)),
                  pl.BlockSpec((1, 1, past, MLA_KV_LORA), lambda b: (layer, b, 0, 0)),
                  pl.BlockSpec((1, 1, past, MLA_ROPE), lambda b: (layer, b, 0, 0)),
                  pl.BlockSpec((MLA_HEADS, MLA_NOPE, MLA_KV_LORA), lambda b: (0, 0, 0)),
                  pl.BlockSpec((MLA_HEADS, MLA_KV_LORA, MLA_V), lambda b: (0, 0, 0))],
        out_specs=pl.BlockSpec((tq, w), lambda b: (b, 0)),
        out_shape=jax.ShapeDtypeStruct((batch * tq, w), BF16),
        compiler_params=_cparams(("parallel",)),
        name="mla_attn_sample",
    )(qn, qr, ckv_new, kr_new, cache_ckv, cache_kr, w_uk_t, w_uv)


def _cross_body(q_ref, k_ref, v_ref, o_ref, *, groups, tq):
    for g in range(groups):
        rows = slice(g * tq, (g + 1) * tq)
        for h in range(MEM_HEADS):
            sl = slice(h * HEAD_DIM, (h + 1) * HEAD_DIM)
            s = _qk(q_ref[rows, sl], k_ref[g, :, sl].astype(BF16))
            p = jnp.exp2(s - jnp.max(s, axis=1, keepdims=True))
            l = jnp.sum(p, axis=1, keepdims=True)
            o = jnp.dot(p.astype(BF16), v_ref[g, :, sl].astype(BF16), preferred_element_type=F32)
            o_ref[rows, sl] = (o / l).astype(o_ref.dtype)


def _cross_attn(q, mem_k, mem_v, *, row0, n_rows, rows_per_stream, groups, tq):
    wq = MEM_HEADS * HEAD_DIM
    n_mem = mem_k.shape[1]
    step_rows = groups * tq
    steps_per_stream_group = rows_per_stream // tq
    rb0 = row0 // step_rows
    mem_spec = pl.BlockSpec((groups, n_mem, wq), lambda i: (i // steps_per_stream_group, 0, 0))
    return pl.pallas_call(
        functools.partial(_cross_body, groups=groups, tq=tq),
        grid=(n_rows // step_rows,),
        in_specs=[pl.BlockSpec((step_rows, wq), lambda i: (rb0 + i, 0)), mem_spec, mem_spec],
        out_specs=pl.BlockSpec((step_rows, wq), lambda i: (i, 0)),
        out_shape=jax.ShapeDtypeStruct((n_rows, wq), BF16),
        compiler_params=_cparams(("parallel",)),
        name="cross_attn",
    )(q, mem_k, mem_v)


def _split_bf16(x):
    hi = x.astype(BF16)
    return hi, (x - hi.astype(F32)).astype(BF16)


def _router_body(x_ref, g_ref, w_ref, h_ref, idx_ref, gate_ref):
    h = _rms(x_ref[...], g_ref[...])
    h_ref[...] = h
    h_hi, h_lo = _split_bf16(h)
    w_hi, w_lo = _split_bf16(w_ref[...])
    logits = (jnp.dot(h_hi, w_hi, preferred_element_type=F32)
              + jnp.dot(h_hi, w_lo, preferred_element_type=F32)
              + jnp.dot(h_lo, w_hi, preferred_element_type=F32))
    lane = lax.broadcasted_iota(jnp.int32, logits.shape, 1)
    logits = jnp.where(lane < N_EXPERTS, logits, NEG_BIG)
    v1 = jnp.max(logits, axis=1, keepdims=True)
    i1 = jnp.min(jnp.where(logits == v1, lane, LANES), axis=1, keepdims=True)
    rest = jnp.where(lane == i1, NEG_BIG, logits)
    v2 = jnp.max(rest, axis=1, keepdims=True)
    i2 = jnp.min(jnp.where(rest == v2, lane, LANES), axis=1, keepdims=True)
    e2 = jnp.exp(v2 - v1)
    g1 = 1.0 / (1.0 + e2)
    g2 = e2 / (1.0 + e2)
    idx_ref[...] = jnp.where(lane == 0, i1, jnp.where(lane == 1, i2, 0))
    gate_ref[...] = jnp.where(lane == 0, g1, jnp.where(lane == 1, g2, 0.0))


def _router(x, gain, w_router, *, tm=512):
    n, d = x.shape
    assert n % tm == 0
    return pl.pallas_call(
        _router_body,
        grid=(n // tm,),
        in_specs=[pl.BlockSpec((tm, d), lambda i: (i, 0)),
                  pl.BlockSpec((1, d), lambda i: (0, 0)),
                  pl.BlockSpec((d, LANES), lambda i: (0, 0))],
        out_specs=[pl.BlockSpec((tm, d), lambda i: (i, 0)),
                   pl.BlockSpec((tm, LANES), lambda i: (i, 0)),
                   pl.BlockSpec((tm, LANES), lambda i: (i, 0))],
        out_shape=[jax.ShapeDtypeStruct((n, d), F32),
                   jax.ShapeDtypeStruct((n, LANES), jnp.int32),
                   jax.ShapeDtypeStruct((n, LANES), F32)],
        compiler_params=_cparams(("parallel",)),
        name="router",
    )(x, gain.reshape(1, d).astype(F32), _pad_lanes(w_router))


ROW_DMA_UNROLL = 8


def _gather_body(idx_ref, src_ref, out_ref, buf_ref, sem, *, rows):
    base = pl.program_id(0) * rows

    def issue(g, carry):
        for u in range(ROW_DMA_UNROLL):
            r = g * ROW_DMA_UNROLL + u
            pltpu.make_async_copy(src_ref.at[pl.ds(idx_ref[base + r], 1)], buf_ref.at[pl.ds(r, 1)], sem).start()
        return carry

    lax.fori_loop(0, rows // ROW_DMA_UNROLL, issue, 0)
    pltpu.make_async_copy(src_ref.at[pl.ds(0, rows)], buf_ref, sem).wait()
    out_ref[...] = buf_ref[...].astype(out_ref.dtype)


def _gather_rows(src, row_idx, *, out_dtype, rows=512):
    r, d = row_idx.shape[0], src.shape[1]
    assert r % rows == 0 and rows % ROW_DMA_UNROLL == 0 and src.dtype == F32
    return pl.pallas_call(
        functools.partial(_gather_body, rows=rows),
        grid_spec=pltpu.PrefetchScalarGridSpec(
            num_scalar_prefetch=1, grid=(r // rows,),
            in_specs=[pl.BlockSpec(memory_space=pl.ANY)],
            out_specs=pl.BlockSpec((rows, d), lambda i, idx: (i, 0)),
            scratch_shapes=[pltpu.VMEM((rows, d), F32), pltpu.SemaphoreType.DMA(())]),
        out_shape=jax.ShapeDtypeStruct((r, d), out_dtype),
        compiler_params=_cparams(("arbitrary",)),
        name="gather_rows",
    )(row_idx, src)


def _moe_body(wt_ref, we_ref, ws_ref, wc_ref, wf_ref, x_ref, wg_ref, wu_ref, wd_ref, out_ref, *, ts, nsub):
    w, f = pl.program_id(0), pl.program_id(1)

    @pl.when((f == 0) & (wf_ref[w] == 1))
    def _():
        out_ref[...] = jnp.zeros(out_ref.shape, F32)

    s0, cnt = ws_ref[w], wc_ref[w]
    for a in range(nsub):
        for c in range(1, nsub - a + 1):
            @pl.when((s0 == a) & (cnt == c))
            def _():
                rows = slice(a * ts, (a + c) * ts)
                out_ref[rows, :] += _swiglu_partial(x_ref[rows, :], wg_ref[...], wu_ref[...], wd_ref[...])


def _moe_ffn(xs, w_gu, w_d, layer, items, *, tm, ts, tf=256):
    r, d = xs.shape
    f = w_d.shape[2]
    nf = f // tf
    n_items = items[0].shape[0]
    assert r % tm == 0 and tm % ts == 0 and f % tf == 0

    def fidx(w, j, wc):
        return jnp.where(wc[w] > 0, j, nf - 1)

    return pl.pallas_call(
        functools.partial(_moe_body, ts=ts, nsub=tm // ts),
        grid_spec=pltpu.PrefetchScalarGridSpec(
            num_scalar_prefetch=5, grid=(n_items, nf),
            in_specs=[pl.BlockSpec((tm, d), lambda w, j, wt, we, ws, wc, wf: (wt[w], 0)),
                      pl.BlockSpec((None, None, d, tf),
                                   lambda w, j, wt, we, ws, wc, wf: (layer, we[w], 0, fidx(w, j, wc))),
                      pl.BlockSpec((None, None, d, tf),
                                   lambda w, j, wt, we, ws, wc, wf: (layer, we[w], 0, nf + fidx(w, j, wc))),
                      pl.BlockSpec((None, None, tf, d),
                                   lambda w, j, wt, we, ws, wc, wf: (layer, we[w], fidx(w, j, wc), 0))],
            out_specs=pl.BlockSpec((tm, d), lambda w, j, wt, we, ws, wc, wf: (wt[w], 0))),
        out_shape=jax.ShapeDtypeStruct((r, d), F32),
        compiler_params=_cparams(("arbitrary", "arbitrary")),
        name="moe_ffn",
    )(*items, xs, w_gu, w_gu, w_d)


def _combine_body(p0_ref, p1_ref, x_ref, gate_ref, ys_ref, out_ref, buf0_ref, buf1_ref, sem, *, rows):
    base = pl.program_id(0) * rows

    def issue(g, carry):
        for u in range(ROW_DMA_UNROLL):
            r = g * ROW_DMA_UNROLL + u
            pltpu.make_async_copy(ys_ref.at[pl.ds(p0_ref[base + r], 1)], buf0_ref.at[pl.ds(r, 1)], sem).start()
            pltpu.make_async_copy(ys_ref.at[pl.ds(p1_ref[base + r], 1)], buf1_ref.at[pl.ds(r, 1)], sem).start()
        return carry

    lax.fori_loop(0, rows // ROW_DMA_UNROLL, issue, 0)
    pltpu.make_async_copy(ys_ref.at[pl.ds(0, rows)], buf0_ref, sem).wait()
    pltpu.make_async_copy(ys_ref.at[pl.ds(0, rows)], buf1_ref, sem).wait()
    g = gate_ref[...]
    out_ref[...] = x_ref[...] + (g[:, 0:1] * buf0_ref[...] + g[:, 1:2] * buf1_ref[...])


def _moe_combine(x, gates, ys, pos0, pos1, *, rows=256):
    n, d = x.shape
    assert n % rows == 0 and rows % ROW_DMA_UNROLL == 0
    return pl.pallas_call(
        functools.partial(_combine_body, rows=rows),
        grid_spec=pltpu.PrefetchScalarGridSpec(
            num_scalar_prefetch=2, grid=(n // rows,),
            in_specs=[pl.BlockSpec((rows, d), lambda i, p0, p1: (i, 0)),
                      pl.BlockSpec((rows, LANES), lambda i, p0, p1: (i, 0)),
                      pl.BlockSpec(memory_space=pl.ANY)],
            out_specs=pl.BlockSpec((rows, d), lambda i, p0, p1: (i, 0)),
            scratch_shapes=[pltpu.VMEM((rows, d), F32), pltpu.VMEM((rows, d), F32),
                            pltpu.SemaphoreType.DMA(())]),
        out_shape=jax.ShapeDtypeStruct((n, d), F32),
        compiler_params=_cparams(("arbitrary",)),
        name="moe_combine",
    )(pos0, pos1, x, gates, ys)


def _moe_plan(idx, *, tm, ts):
    n = idx.shape[0]
    nsub = tm // ts
    cap = -(-(2 * n + N_EXPERTS * ts) // tm) * tm
    n_tiles, n_subs = cap // tm, cap // ts
    n_items = n_tiles + N_EXPERTS - 1
    e_flat = idx.reshape(-1)
    onehot = (e_flat[:, None] == jnp.arange(N_EXPERTS, dtype=jnp.int32)[None, :]).astype(jnp.int32)
    before = jnp.cumsum(onehot, axis=0) - onehot
    rank = jnp.sum(before * onehot, axis=1)
    counts = jnp.sum(onehot, axis=0)
    padded = ((counts + ts - 1) // ts) * ts
    g_end = jnp.cumsum(padded)
    g_start = g_end - padded
    pos = g_start[e_flat] + rank
    row_token = jnp.zeros((cap,), jnp.int32).at[pos].set(jnp.arange(2 * n, dtype=jnp.int32) // 2)
    s = jnp.arange(n_subs, dtype=jnp.int32)
    sub_e = jnp.sum((s[:, None] * ts >= g_end[None, :]).astype(jnp.int32), axis=1)
    valid = sub_e < N_EXPERTS
    prev_e = jnp.concatenate([jnp.full((1,), -1, jnp.int32), sub_e[:-1]])
    tile_start = (s % nsub) == 0
    new = tile_start | (valid & (sub_e != prev_e))
    item_of_sub = jnp.cumsum(new.astype(jnp.int32)) - 1
    n_used = item_of_sub[-1] + 1
    tgt = jnp.where(new, item_of_sub, n_items)
    w = jnp.arange(n_items, dtype=jnp.int32)
    used = w < n_used
    it_tile = jnp.full((n_items,), n_tiles - 1, jnp.int32).at[tgt].set(s // nsub, mode="drop")
    it_exp = jnp.full((n_items,), N_EXPERTS - 1, jnp.int32).at[tgt].set(
        jnp.minimum(sub_e, N_EXPERTS - 1), mode="drop")
    it_s0 = jnp.zeros((n_items,), jnp.int32).at[tgt].set(s % nsub, mode="drop")
    it_first = jnp.zeros((n_items,), jnp.int32).at[tgt].set(tile_start.astype(jnp.int32), mode="drop")
    it_cnt = jnp.zeros((n_items,), jnp.int32).at[item_of_sub].add(valid.astype(jnp.int32), mode="drop")
    it_cnt = jnp.where(used, it_cnt, 0)
    return pos[0::2], pos[1::2], row_token, (it_tile, it_exp, it_s0, it_cnt, it_first)


def _moe_layer(x, gain, w_router, w_gu, w_d, layer, *, tm=1024, ts=256):
    h, idx, gates = _router(x, gain, w_router)
    pos0, pos1, row_token, items = _moe_plan(idx[:, :2], tm=tm, ts=ts)
    xs = _gather_rows(h, row_token, out_dtype=BF16)
    ys = _moe_ffn(xs, w_gu, w_d, layer, items, tm=tm, ts=ts)
    return _moe_combine(x, gates, ys, pos0, pos1)


def _fox_layer(x, gain, w_in, b_f, w_o, cache_k, cache_v, cache_logf, layer, *, bp, tp, bs, tq):
    n_p = bp * tp
    w = FOX_HEADS * HEAD_DIM
    q_scale = jnp.full((w,), HEAD_DIM ** -0.5 * LOG2E, F32)
    q = _matmul(x, w_in, layer=layer, gain=gain, col_scale=q_scale, m=w, out_dtypes=(BF16,))
    kv, kv_b = _matmul(x, w_in, layer=layer, gain=gain, w_col0=w, m=2 * w, out_dtypes=(F32, BF16))
    logf = _matmul(x, _pad_lanes(w_in[layer, :, 3 * w:]), gain=gain, bias=_pad_lanes(b_f),
                   act="log_sigmoid")[:, :FOX_HEADS]
    logf_p = logf[:n_p].reshape(bp, tp, FOX_HEADS)
    logf_s = logf[n_p:].reshape(bs, tq, FOX_HEADS)
    past = cache_logf.shape[2]
    cs_p = _cumsum_time(logf_p)
    cs_s = _cumsum_time(jnp.concatenate([cache_logf[layer], logf_s], axis=1))
    aq, ak = _fox_aug(cs_p.reshape(n_p, FOX_HEADS))
    o_p = _fox_attn_prompt(q, kv_b, aq, ak, batch=bp, seq=tp)
    o_s = _fox_attn_sample(q, kv_b, cache_k, cache_v, layer, cs_s[:, past:], cs_s.transpose(0, 2, 1),
                           row0=n_p, batch=bs, tq=tq)
    x = _matmul(jnp.concatenate([o_p, o_s], axis=0), w_o, layer=layer, res=x)
    return x, kv[:, :w], kv[:, w:], logf


def _mla_layer(x, gain, w_in, g_q, g_kv, w_uq, w_ukv, w_o, layer, cache_ckv, cache_kr, cos, sin,
               *, bp, tp, bs, tq):
    n_p = bp * tp
    h, c, r = MLA_HEADS, MLA_KV_LORA, MLA_ROPE
    w_kr = w_in[:, MLA_Q_LORA + c:]
    w_in_big = jnp.concatenate([w_in[:, :MLA_Q_LORA + c], _pad_lanes(w_kr), _pad_lanes(_swap_halves(w_kr))], axis=1)
    w_q = w_uq.reshape(MLA_Q_LORA, h, MLA_NOPE + r)
    w_qr = w_q[:, :, MLA_NOPE:]
    w_uq_big = jnp.concatenate([w_q[:, :, :MLA_NOPE].reshape(MLA_Q_LORA, h * MLA_NOPE),
                                _pad_lanes(w_qr).reshape(MLA_Q_LORA, h * LANES),
                                _pad_lanes(_swap_halves(w_qr)).reshape(MLA_Q_LORA, h * LANES)], axis=1)
    w_kv = w_ukv.reshape(c, h, MLA_NOPE + MLA_V)
    w_ukv_split = jnp.concatenate([w_kv[:, :, :MLA_NOPE].reshape(c, h * MLA_NOPE),
                                   w_kv[:, :, MLA_NOPE:].reshape(c, h * MLA_V)], axis=1)
    w_uk_t = w_kv[:, :, :MLA_NOPE].transpose(1, 2, 0).astype(BF16)
    w_uv = w_kv[:, :, MLA_NOPE:].transpose(1, 0, 2).astype(BF16)
    q_scale = (MLA_NOPE + r) ** -0.5 * LOG2E

    a = _matmul(x, w_in_big, gain=gain, tn=256)
    ckv, ckv_b = _rmsnorm(a, g_kv, x_col=1, k=c, out_dtypes=(F32, BF16))
    kr, kr_b = _rope(a, cos, sin, x_col=(MLA_Q_LORA + c) // (2 * LANES), width=LANES, out_dtypes=(F32, BF16))
    qn = _matmul(a, w_uq_big, gain=g_q, x_col=0, k=MLA_Q_LORA, m=h * MLA_NOPE,
                 col_scale=jnp.full((h * MLA_NOPE,), q_scale, F32), out_dtypes=(BF16,))
    qr2 = _matmul(a, w_uq_big, gain=g_q, x_col=0, k=MLA_Q_LORA, w_col0=h * MLA_NOPE,
                  col_scale=jnp.full((2 * h * LANES,), q_scale, F32))
    qr = _rope(qr2, cos, sin, width=h * LANES, out_dtypes=(BF16,))
    kv = _matmul(ckv_b, w_ukv_split, n_rows=n_p, out_dtypes=(BF16,))
    o_p = _mla_attn_prompt(qn, qr, kv, kr_b, batch=bp, seq=tp)
    o_s = _mla_attn_sample(qn, qr, ckv_b, kr_b, cache_ckv, cache_kr, layer, w_uk_t, w_uv,
                           row0=n_p, batch=bs, tq=tq)
    x = _matmul(jnp.concatenate([o_p, o_s], axis=0), w_o, layer=layer, res=x)
    return x, ckv, kr[:, :r]


def _cross_layer(x, gain, w_q, w_o, layer, mem_p, mem_s, *, bp, tp, bs, tq):
    n_p = bp * tp
    wq = MEM_HEADS * HEAD_DIM
    q = _matmul(x, w_q, layer=layer, gain=gain, col_scale=jnp.full((wq,), HEAD_DIM ** -0.5 * LOG2E, F32),
                out_dtypes=(BF16,))
    o_p = _cross_attn(q, mem_p[0], mem_p[1], row0=0, n_rows=n_p, rows_per_stream=tp, groups=1, tq=1024)
    o_s = _cross_attn(q, mem_s[0], mem_s[1], row0=n_p, n_rows=bs * tq, rows_per_stream=tq, groups=8, tq=tq)
    return _matmul(jnp.concatenate([o_p, o_s], axis=0), w_o, layer=layer, res=x)


def kernel(x_prompt, x_sample, cache_fox_k, cache_fox_v, cache_fox_logf, cache_mla_ckv, cache_mla_krope, cache_mem_k, cache_mem_v, mem_prompt, norm_mix, norm_cross, norm_ffn, norm_final, fox_w_in, fox_b_f, fox_w_o, mla_w_in, mla_g_q, mla_g_kv, mla_w_uq, mla_w_ukv, mla_w_o, cross_w_q, cross_w_kv, cross_w_o, ffn_w_gu, ffn_w_d, moe_w_router, moe_w_gu, moe_w_d):
    bp, tp, d = x_prompt.shape
    bs, tq, _ = x_sample.shape
    n_p = bp * tp
    depth = norm_mix.shape[0]
    past = cache_fox_k.shape[2]
    n_mem = mem_prompt.shape[1]
    mem_w = MEM_HEADS * HEAD_DIM
    dims = dict(bp=bp, tp=tp, bs=bs, tq=tq)

    x = jnp.concatenate([x_prompt.reshape(n_p, d), x_sample.reshape(bs * tq, d)], axis=0)
    pos = np.concatenate([np.tile(np.arange(tp), bp), np.tile(past + np.arange(tq), bs)])
    cos, sin = _rope_tables(pos)
    mem_flat = mem_prompt.reshape(bp * n_mem, d)

    fox_rows, mla_rows, mem_rows = [], [], []
    for i in range(depth):
        j = i // 2
        if i % 2 == 0:
            x, k, v, lf = _fox_layer(x, norm_mix[i], fox_w_in, fox_b_f[j], fox_w_o,
                                     cache_fox_k, cache_fox_v, cache_fox_logf, j, **dims)
            fox_rows.append((k, v, lf))
        else:
            x, ckv, kr = _mla_layer(x, norm_mix[i], mla_w_in[j], mla_g_q[j], mla_g_kv[j], mla_w_uq[j],
                                    mla_w_ukv[j], mla_w_o, j, cache_mla_ckv, cache_mla_krope,
                                    cos, sin, **dims)
            mla_rows.append((ckv, kr))
        kv_p = _matmul(mem_flat, cross_w_kv, layer=i)
        mem_p = (kv_p[:, :mem_w].reshape(bp, n_mem, mem_w), kv_p[:, mem_w:].reshape(bp, n_mem, mem_w))
        mem_rows.append(mem_p)
        mem_s = (cache_mem_k[i].reshape(bs, n_mem, mem_w), cache_mem_v[i].reshape(bs, n_mem, mem_w))
        x = _cross_layer(x, norm_cross[i], cross_w_q, cross_w_o, i, mem_p, mem_s, **dims)
        if i % 2 == 0:
            x = _ffn(x, norm_ffn[i], ffn_w_gu, ffn_w_d, j)
        else:
            x = _moe_layer(x, norm_ffn[i], moe_w_router[j], moe_w_gu, moe_w_d, j)
    y = _rmsnorm(x, norm_final)

    def split(rows, tail):
        return rows[:n_p].reshape((bp, tp) + tail), rows[n_p:].reshape((bs, tq) + tail)

    def stacked(rows_list, tail):
        parts = [split(r, tail) for r in rows_list]
        return jnp.stack([p[0] for p in parts]), jnp.stack([p[1] for p in parts])

    y_p, y_s = split(y, (d,))
    hd = (FOX_HEADS, HEAD_DIM)
    fox_k_p, fox_k_s = stacked([r[0] for r in fox_rows], hd)
    fox_v_p, fox_v_s = stacked([r[1] for r in fox_rows], hd)
    fox_lf_p, fox_lf_s = stacked([r[2] for r in fox_rows], (FOX_HEADS,))
    mla_ckv_p, mla_ckv_s = stacked([r[0] for r in mla_rows], (MLA_KV_LORA,))
    mla_kr_p, mla_kr_s = stacked([r[1] for r in mla_rows], (MLA_ROPE,))
    mem_k_p = jnp.stack([m[0].reshape(bp, n_mem, MEM_HEADS, HEAD_DIM) for m in mem_rows])
    mem_v_p = jnp.stack([m[1].reshape(bp, n_mem, MEM_HEADS, HEAD_DIM) for m in mem_rows])
    return (y_p, y_s, fox_k_p, fox_v_p, fox_lf_p, mla_ckv_p, mla_kr_p, mem_k_p, mem_v_p,
            fox_k_s, fox_v_s, fox_lf_s, mla_ckv_s, mla_kr_s)
```

```python
import functools

import numpy as np
import jax
import jax.numpy as jnp
from jax import lax
from jax.experimental import pallas as pl
from jax.experimental.pallas import tpu as pltpu

F32 = jnp.float32
BF16 = jnp.bfloat16

D_MODEL = 2048
CHUNK = 64
RMS_EPS = 1e-6
FOX_HEADS = 16
HEAD_DIM = 128
MLA_HEADS = 16
MLA_Q_LORA = 512
MLA_KV_LORA = 512
MLA_NOPE = 128
MLA_ROPE = 64
MLA_V = 128
ROPE_BASE = 10000.0
MEM_HEADS = 4
N_EXPERTS = 8
LANES = 128

VMEM_LIMIT_BYTES = 56 * 1024 * 1024
NEG_BIG = -1e30
LOG2E = 1.4426950408889634


def _cparams(sem):
    return pltpu.CompilerParams(dimension_semantics=sem, vmem_limit_bytes=VMEM_LIMIT_BYTES)


def _rms(xf, g):
    return xf * lax.rsqrt(jnp.mean(xf * xf, axis=-1, keepdims=True) + RMS_EPS) * g


def _log_sigmoid(x):
    return jnp.minimum(x, 0.0) - jnp.log1p(jnp.exp(-jnp.abs(x)))


def _split3_bf16(x):
    hi = x.astype(BF16)
    r1 = x - hi.astype(F32)
    mid = r1.astype(BF16)
    lo = (r1 - mid.astype(F32)).astype(BF16)
    return hi, mid, lo


def _mm_body(*refs, has_norm, has_bias, has_scale, has_res, act, use_scratch, n_out, split):
    it = iter(refs)
    x_ref = next(it)
    x2_ref = next(it) if split is not None else None
    g_ref = next(it) if has_norm else None
    w_ref = next(it)
    b_ref = next(it) if has_bias else None
    s_ref = next(it) if has_scale else None
    r_ref = next(it) if has_res else None
    out_refs = [next(it) for _ in range(n_out)]
    xb_ref = next(it) if use_scratch else None

    def stage(src_ref):
        xf = src_ref[...].astype(F32)
        if has_norm:
            xf = _rms(xf, g_ref[...])
        xb_ref[...] = xf.astype(BF16)

    if use_scratch:
        first = pl.program_id(1) == 0
        if split is None:
            pl.when(first)(lambda: stage(x_ref))
        else:
            pl.when(first & (pl.program_id(0) < split))(lambda: stage(x_ref))
            pl.when(first & (pl.program_id(0) >= split))(lambda: stage(x2_ref))
        xb = xb_ref[...]
    else:
        xb = x_ref[...]
    acc = jnp.dot(xb, w_ref[...].astype(BF16), preferred_element_type=F32)
    if has_bias:
        acc = acc + b_ref[...]
    if act == "log_sigmoid":
        acc = _log_sigmoid(acc)
    if has_scale:
        acc = acc * s_ref[...]
    if has_res:
        acc = acc + r_ref[...]
    for o in out_refs:
        o[...] = acc.astype(o.dtype)


def _matmul(x, w, *, layer=None, gain=None, bias=None, col_scale=None, res=None, act=None, x_col=0,
            k=None, w_col0=0, m=None, n_rows=None, out_dtypes=(F32,), tm=1024, tn=1024):
    x, x2 = x if isinstance(x, tuple) else (x, None)
    n = (x.shape[0] + (0 if x2 is None else x2.shape[0])) if n_rows is None else n_rows
    k = x.shape[1] if k is None else k
    m = w.shape[-1] - w_col0 if m is None else m
    tm = min(tm, n)
    tn = next(c for c in (tn, 512, 256, 128, m) if m % c == 0 and w_col0 % c == 0)
    assert n % tm == 0 and w.shape[-2] == k
    assert (w.ndim == 3) == (layer is not None)
    has_norm, has_bias, has_res = gain is not None, bias is not None, res is not None
    has_scale = col_scale is not None
    use_scratch = has_norm or x.dtype != BF16 or x2 is not None
    col0 = w_col0 // tn
    row_vec = pl.BlockSpec((1, tn), lambda i, j: (0, j))
    if x2 is None:
        split = None
        in_specs = [pl.BlockSpec((tm, k), lambda i, j: (i, x_col))]
        args = [x]
    else:
        assert x.shape[0] % tm == 0 and x2.shape[0] % tm == 0 and n_rows is None
        split = x.shape[0] // tm
        in_specs = [pl.BlockSpec((tm, k), lambda i, j: (jnp.minimum(i, split - 1), x_col)),
                    pl.BlockSpec((tm, k), lambda i, j: (jnp.maximum(i - split, 0), x_col))]
        args = [x, x2]
    if has_norm:
        in_specs.append(pl.BlockSpec((1, k), lambda i, j: (0, 0)))
        args.append(gain.reshape(1, k).astype(F32))
    if layer is None:
        in_specs.append(pl.BlockSpec((k, tn), lambda i, j: (0, j + col0)))
    else:
        in_specs.append(pl.BlockSpec((None, k, tn), lambda i, j: (layer, 0, j + col0)))
    args.append(w)
    if has_bias:
        in_specs.append(row_vec)
        args.append(bias.reshape(1, m).astype(F32))
    if has_scale:
        in_specs.append(row_vec)
        args.append(col_scale.reshape(1, m).astype(F32))
    if has_res:
        in_specs.append(pl.BlockSpec((tm, tn), lambda i, j: (i, j)))
        args.append(res)
    body = functools.partial(_mm_body, has_norm=has_norm, has_bias=has_bias, has_scale=has_scale,
                             has_res=has_res, act=act, use_scratch=use_scratch, n_out=len(out_dtypes),
                             split=split)
    outs = pl.pallas_call(
        body,
        grid=(n // tm, m // tn),
        in_specs=in_specs,
        out_specs=[pl.BlockSpec((tm, tn), lambda i, j: (i, j)) for _ in out_dtypes],
        out_shape=[jax.ShapeDtypeStruct((n, m), dt) for dt in out_dtypes],
        scratch_shapes=[pltpu.VMEM((tm, k), BF16)] if use_scratch else [],
        compiler_params=_cparams(("parallel", "arbitrary")),
        name="matmul",
    )(*args)
    return outs[0] if len(out_dtypes) == 1 else outs


def _norm_body(x_ref, g_ref, *out_refs):
    y = _rms(x_ref[...].astype(F32), g_ref[...])
    for o in out_refs:
        o[...] = y.astype(o.dtype)


def _rmsnorm(x, gain, *, x_col=0, k=None, row0=0, n_rows=None, out_dtypes=(F32,), tm=1024):
    n = x.shape[0] - row0 if n_rows is None else n_rows
    k = x.shape[1] if k is None else k
    tm = min(tm, n)
    assert n % tm == 0 and row0 % tm == 0
    rb0 = row0 // tm
    outs = pl.pallas_call(
        _norm_body,
        grid=(n // tm,),
        in_specs=[pl.BlockSpec((tm, k), lambda i: (rb0 + i, x_col)),
                  pl.BlockSpec((1, k), lambda i: (0, 0))],
        out_specs=[pl.BlockSpec((tm, k), lambda i: (i, 0)) for _ in out_dtypes],
        out_shape=[jax.ShapeDtypeStruct((n, k), dt) for dt in out_dtypes],
        compiler_params=_cparams(("parallel",)),
        name="rmsnorm",
    )(x, gain.reshape(1, k).astype(F32))
    return outs[0] if len(out_dtypes) == 1 else outs


def _swiglu_partial(xb, wg, wu, wd):
    g = jnp.dot(xb, wg.astype(BF16), preferred_element_type=F32)
    u = jnp.dot(xb, wu.astype(BF16), preferred_element_type=F32)
    a = (g * jax.nn.sigmoid(g)) * u
    return jnp.dot(a.astype(BF16), wd.astype(BF16), preferred_element_type=F32)


def _ffn_body(x_ref, g_ref, wg_ref, wu_ref, wd_ref, out_ref, xb_ref):
    @pl.when(pl.program_id(1) == 0)
    def _():
        xf = x_ref[...]
        xb_ref[...] = _rms(xf, g_ref[...]).astype(BF16)
        out_ref[...] = xf
    out_ref[...] += _swiglu_partial(xb_ref[...], wg_ref[...], wu_ref[...], wd_ref[...])


def _ffn(x, gain, w_gu, w_d, layer, *, tm=1024, tf=256):
    n, d = x.shape
    f = w_d.shape[1]
    assert n % tm == 0 and f % tf == 0
    nf = f // tf
    return pl.pallas_call(
        _ffn_body,
        grid=(n // tm, nf),
        in_specs=[pl.BlockSpec((tm, d), lambda i, j: (i, 0)),
                  pl.BlockSpec((1, d), lambda i, j: (0, 0)),
                  pl.BlockSpec((None, d, tf), lambda i, j: (layer, 0, j)),
                  pl.BlockSpec((None, d, tf), lambda i, j: (layer, 0, j + nf)),
                  pl.BlockSpec((None, tf, d), lambda i, j: (layer, j, 0))],
        out_specs=pl.BlockSpec((tm, d), lambda i, j: (i, 0)),
        out_shape=jax.ShapeDtypeStruct((n, d), F32),
        scratch_shapes=[pltpu.VMEM((tm, d), BF16)],
        compiler_params=_cparams(("parallel", "arbitrary")),
        name="ffn",
    )(x, gain.reshape(1, d).astype(F32), w_gu, w_gu, w_d)


def _cumsum_body(x_ref, out_ref, *, chunk):
    t = x_ref.shape[1]
    row = lax.broadcasted_iota(jnp.int32, (chunk, chunk), 0)
    col = lax.broadcasted_iota(jnp.int32, (chunk, chunk), 1)
    tri = jnp.where(col <= row, 1.0, 0.0).astype(BF16)
    carry = jnp.zeros((1, x_ref.shape[2]), F32)
    for c in range(t // chunk):
        hi, mid, lo = _split3_bf16(x_ref[0, c * chunk:(c + 1) * chunk, :])
        cs = (jnp.dot(tri, hi, preferred_element_type=F32)
              + jnp.dot(tri, mid, preferred_element_type=F32)
              + jnp.dot(tri, lo, preferred_element_type=F32)) + carry
        out_ref[0, c * chunk:(c + 1) * chunk, :] = cs
        carry = cs[chunk - 1:chunk, :]


def _cumsum_time(x):
    b, t, h = x.shape
    chunk = next(c for c in (256, 128, 64, 32, 16, 8) if t % c == 0)
    return pl.pallas_call(
        functools.partial(_cumsum_body, chunk=chunk),
        grid=(b,),
        in_specs=[pl.BlockSpec((1, t, h), lambda i: (i, 0, 0))],
        out_specs=pl.BlockSpec((1, t, h), lambda i: (i, 0, 0)),
        out_shape=jax.ShapeDtypeStruct((b, t, h), F32),
        compiler_params=_cparams(("parallel",)),
        name="cumsum",
    )(x)


def _qk(q, k):
    return lax.dot_general(q, k, (((1,), (1,)), ((), ())), preferred_element_type=F32)


def _flash_step(s, v, m_ref, l_ref, acc_ref, h, lanes):
    m_old = m_ref[h]
    m_new = jnp.maximum(m_old, jnp.max(s, axis=1, keepdims=True))
    alpha = jnp.exp2(m_old - m_new)
    p = jnp.exp2(s - jnp.concatenate([m_new] * (s.shape[1] // LANES), axis=1))
    l_ref[h] = alpha * l_ref[h] + jnp.sum(p, axis=1, keepdims=True)
    acc_ref[:, lanes] = alpha * acc_ref[:, lanes] + jnp.dot(p.astype(BF16), v, preferred_element_type=F32)
    m_ref[h] = m_new


def _flash_init(m_ref, l_ref, acc_ref):
    m_ref[...] = jnp.full(m_ref.shape, NEG_BIG, F32)
    l_ref[...] = jnp.zeros(l_ref.shape, F32)
    acc_ref[...] = jnp.zeros(acc_ref.shape, F32)


def _flash_finish(o_ref, l_ref, acc_ref, heads):
    for h in range(heads):
        sl = slice(h * LANES, (h + 1) * LANES)
        o_ref[:, sl] = (acc_ref[:, sl] / l_ref[h]).astype(o_ref.dtype)


def _flash_scratch(t, heads):
    return [pltpu.VMEM((heads, t, LANES), F32), pltpu.VMEM((heads, t, LANES), F32),
            pltpu.VMEM((t, heads * LANES), F32)]


def _fox_aug_body(cs_ref, aq_ref, ak_ref):
    c = cs_ref[...] * LOG2E
    shape = (c.shape[0], LANES)
    lane = lax.broadcasted_iota(jnp.int32, shape, 1)
    for h in range(FOX_HEADS):
        hi, mid, lo = (p.astype(F32) for p in _split3_bf16(jnp.broadcast_to(c[:, h:h + 1], shape)))
        aq = jnp.where(lane == 0, hi, jnp.where(lane == 1, mid, jnp.where(lane == 2, lo,
                       jnp.where(lane < 6, 1.0, 0.0))))
        ak = jnp.where(lane < 3, 1.0, jnp.where(lane == 3, -hi, jnp.where(lane == 4, -mid,
                       jnp.where(lane == 5, -lo, 0.0))))
        sl = slice(h * LANES, (h + 1) * LANES)
        aq_ref[:, sl] = aq.astype(BF16)
        ak_ref[:, sl] = ak.astype(BF16)


def _fox_aug(csum_rows, *, tm=512):
    n, h = csum_rows.shape
    assert n % tm == 0
    return pl.pallas_call(
        _fox_aug_body,
        grid=(n // tm,),
        in_specs=[pl.BlockSpec((tm, h), lambda i: (i, 0))],
        out_specs=[pl.BlockSpec((tm, h * LANES), lambda i: (i, 0))] * 2,
        out_shape=[jax.ShapeDtypeStruct((n, h * LANES), BF16)] * 2,
        compiler_params=_cparams(("parallel",)),
        name="fox_aug",
    )(csum_rows)


def _fox_prompt_body(q_ref, aq_ref, k_ref, v_ref, ak_ref, o_ref, m_ref, l_ref, acc_ref, *, t):
    qi, ki = pl.program_id(1), pl.program_id(2)

    @pl.when(ki == 0)
    def _():
        _flash_init(m_ref, l_ref, acc_ref)

    def block(diagonal):
        if diagonal:
            allowed = (lax.broadcasted_iota(jnp.int32, (t, t), 1)
                       <= lax.broadcasted_iota(jnp.int32, (t, t), 0))
        for h in range(FOX_HEADS):
            sl = slice(h * LANES, (h + 1) * LANES)
            q = jnp.concatenate([q_ref[:, sl], aq_ref[:, sl]], axis=1)
            k = jnp.concatenate([k_ref[:, sl], ak_ref[:, sl]], axis=1)
            s = _qk(q, k)
            if diagonal:
                s = jnp.where(allowed, s, NEG_BIG)
            _flash_step(s, v_ref[:, sl], m_ref, l_ref, acc_ref, h, sl)

    @pl.when(ki < qi)
    def _():
        block(False)

    @pl.when(ki == qi)
    def _():
        block(True)
        _flash_finish(o_ref, l_ref, acc_ref, FOX_HEADS)


def _fox_attn_prompt(q, kv, aq, ak, *, batch, seq, t=512):
    w = FOX_HEADS * HEAD_DIM
    nt = seq // t
    assert seq % t == 0
    q_row = lambda b, i, j: b * nt + i
    k_row = lambda b, i, j: b * nt + jnp.minimum(i, j)
    return pl.pallas_call(
        functools.partial(_fox_prompt_body, t=t),
        grid=(batch, nt, nt),
        in_specs=[pl.BlockSpec((t, w), lambda b, i, j: (q_row(b, i, j), 0)),
                  pl.BlockSpec((t, w), lambda b, i, j: (q_row(b, i, j), 0)),
                  pl.BlockSpec((t, w), lambda b, i, j: (k_row(b, i, j), 0)),
                  pl.BlockSpec((t, w), lambda b, i, j: (k_row(b, i, j), 1)),
                  pl.BlockSpec((t, w), lambda b, i, j: (k_row(b, i, j), 0))],
        out_specs=pl.BlockSpec((t, w), lambda b, i, j: (q_row(b, i, j), 0)),
        out_shape=jax.ShapeDtypeStruct((batch * seq, w), BF16),
        scratch_shapes=_flash_scratch(t, FOX_HEADS),
        compiler_params=_cparams(("parallel", "parallel", "arbitrary")),
        name="fox_attn_prompt",
    )(q, aq, kv, kv, ak)


def _fox_sample_body(q_ref, kn_ref, vn_ref, kc_ref, vc_ref, cq_ref, ck_ref, o_ref, *, past, tq):
    allowed = (lax.broadcasted_iota(jnp.int32, (tq, tq), 1)
               <= lax.broadcasted_iota(jnp.int32, (tq, tq), 0))
    for h in range(FOX_HEADS):
        sl = slice(h * HEAD_DIM, (h + 1) * HEAD_DIM)
        q = q_ref[:, sl]
        cq = cq_ref[0, :, h:h + 1] * LOG2E
        ck = ck_ref[0, h:h + 1, :] * LOG2E
        kc = kc_ref[0, 0, pl.ds(h, past, stride=FOX_HEADS), :].astype(BF16)
        vc = vc_ref[0, 0, pl.ds(h, past, stride=FOX_HEADS), :].astype(BF16)
        s_c = _qk(q, kc) + (cq - ck[:, :past])
        s_n = _qk(q, kn_ref[:, sl]) + (cq - ck[:, past:])
        s_n = jnp.where(allowed, s_n, NEG_BIG)
        m = jnp.maximum(jnp.max(s_c, axis=1, keepdims=True), jnp.max(s_n, axis=1, keepdims=True))
        p_c = jnp.exp2(s_c - m)
        p_n = jnp.exp2(s_n - m)
        l = jnp.sum(p_c, axis=1, keepdims=True) + jnp.sum(p_n, axis=1, keepdims=True)
        o = (jnp.dot(p_c.astype(BF16), vc, preferred_element_type=F32)
             + jnp.dot(p_n.astype(BF16), vn_ref[:, sl], preferred_element_type=F32))
        o_ref[:, sl] = (o / l).astype(o_ref.dtype)


def _fox_attn_sample(q, kv, cache_k, cache_v, layer, csum_q, csum_t, *, row0, batch, tq):
    w = FOX_HEADS * HEAD_DIM
    n_layers, n_streams, past = cache_k.shape[:3]
    rb0 = row0 // tq
    flat = (n_layers, n_streams, past * FOX_HEADS, HEAD_DIM)
    cache_spec = pl.BlockSpec((1, 1, past * FOX_HEADS, HEAD_DIM), lambda b: (layer, b, 0, 0))
    return pl.pallas_call(
        functools.partial(_fox_sample_body, past=past, tq=tq),
        grid=(batch,),
        in_specs=[pl.BlockSpec((tq, w), lambda b: (rb0 + b, 0)),
                  pl.BlockSpec((tq, w), lambda b: (rb0 + b, 0)),
                  pl.BlockSpec((tq, w), lambda b: (rb0 + b, 1)),
                  cache_spec, cache_spec,
                  pl.BlockSpec((1, tq, FOX_HEADS), lambda b: (b, 0, 0)),
                  pl.BlockSpec((1, FOX_HEADS, past + tq), lambda b: (b, 0, 0))],
        out_specs=pl.BlockSpec((tq, w), lambda b: (b, 0)),
        out_shape=jax.ShapeDtypeStruct((batch * tq, w), BF16),
        compiler_params=_cparams(("parallel",)),
        name="fox_attn_sample",
    )(q, kv, kv, cache_k.reshape(flat), cache_v.reshape(flat), csum_q, csum_t)


def _rope_body(x_ref, cos_ref, sin_ref, *out_refs, width):
    cos, sin = cos_ref[...], sin_ref[...]
    for h in range(width // LANES):
        a = x_ref[:, h * LANES:(h + 1) * LANES]
        b = x_ref[:, width + h * LANES:width + (h + 1) * LANES]
        y = a * cos + b * sin
        for o in out_refs:
            o[:, h * LANES:(h + 1) * LANES] = y.astype(o.dtype)


def _rope(x, cos, sin, *, x_col=0, width, out_dtypes, tm=512):
    n = x.shape[0]
    assert n % tm == 0
    outs = pl.pallas_call(
        functools.partial(_rope_body, width=width),
        grid=(n // tm,),
        in_specs=[pl.BlockSpec((tm, 2 * width), lambda i: (i, x_col)),
                  pl.BlockSpec((tm, LANES), lambda i: (i, 0)),
                  pl.BlockSpec((tm, LANES), lambda i: (i, 0))],
        out_specs=[pl.BlockSpec((tm, width), lambda i: (i, 0)) for _ in out_dtypes],
        out_shape=[jax.ShapeDtypeStruct((n, width), dt) for dt in out_dtypes],
        compiler_params=_cparams(("parallel",)),
        name="rope",
    )(x, cos, sin)
    return outs[0] if len(out_dtypes) == 1 else outs


def _rope_tables(pos):
    half = MLA_ROPE // 2
    inv_freq = ROPE_BASE ** (-np.arange(half, dtype=np.float32) / half)
    ang = jnp.asarray(pos, F32)[:, None] * jnp.asarray(inv_freq)[None, :]
    zeros = jnp.zeros((ang.shape[0], LANES - MLA_ROPE), F32)
    cos = jnp.concatenate([jnp.cos(ang), jnp.cos(ang), zeros], axis=1)
    sin = jnp.concatenate([jnp.sin(ang), jnp.sin(ang), zeros], axis=1)
    return cos, sin


def _swap_halves(w):
    half = w.shape[-1] // 2
    return jnp.concatenate([-w[..., half:], w[..., :half]], axis=-1)


def _pad_lanes(w):
    return jnp.concatenate([w, jnp.zeros(w.shape[:-1] + (LANES - w.shape[-1],), w.dtype)], axis=-1)


def _mla_prompt_body(qn_ref, qr_ref, kn_ref, v_ref, kr_ref, o_ref, m_ref, l_ref, acc_ref, *, t):
    qi, ki = pl.program_id(1), pl.program_id(2)

    @pl.when(ki == 0)
    def _():
        _flash_init(m_ref, l_ref, acc_ref)

    def block(diagonal):
        if diagonal:
            allowed = ((lax.broadcasted_iota(jnp.int32, (t, t), 1) // CHUNK)
                       <= (lax.broadcasted_iota(jnp.int32, (t, t), 0) // CHUNK))
        kr = kr_ref[...]
        for h in range(MLA_HEADS):
            sl = slice(h * LANES, (h + 1) * LANES)
            q = jnp.concatenate([qn_ref[:, sl], qr_ref[:, sl]], axis=1)
            k = jnp.concatenate([kn_ref[:, sl], kr], axis=1)
            s = _qk(q, k)
            if diagonal:
                s = jnp.where(allowed, s, NEG_BIG)
            _flash_step(s, v_ref[:, sl], m_ref, l_ref, acc_ref, h, sl)

    @pl.when(ki < qi)
    def _():
        block(False)

    @pl.when(ki == qi)
    def _():
        block(True)
        _flash_finish(o_ref, l_ref, acc_ref, MLA_HEADS)


def _mla_attn_prompt(qn, qr, kv, kr, *, batch, seq, t=512):
    w = MLA_HEADS * LANES
    nt = seq // t
    assert seq % t == 0 and t % CHUNK == 0
    kv_row = lambda b, i, j: b * nt + jnp.minimum(i, j)
    return pl.pallas_call(
        functools.partial(_mla_prompt_body, t=t),
        grid=(batch, nt, nt),
        in_specs=[pl.BlockSpec((t, w), lambda b, i, j: (b * nt + i, 0)),
                  pl.BlockSpec((t, w), lambda b, i, j: (b * nt + i, 0)),
                  pl.BlockSpec((t, w), lambda b, i, j: (kv_row(b, i, j), 0)),
                  pl.BlockSpec((t, w), lambda b, i, j: (kv_row(b, i, j), 1)),
                  pl.BlockSpec((t, LANES), lambda b, i, j: (kv_row(b, i, j), 0))],
        out_specs=pl.BlockSpec((t, w), lambda b, i, j: (b * nt + i, 0)),
        out_shape=jax.ShapeDtypeStruct((batch * seq, w), BF16),
        scratch_shapes=_flash_scratch(t, MLA_HEADS),
        compiler_params=_cparams(("parallel", "parallel", "arbitrary")),
        name="mla_attn_prompt",
    )(qn, qr, kv, kv, kr)


def _mla_sample_body(qn_ref, qr_ref, cn_ref, krn_ref, cc_ref, krc_ref, wuk_ref, wuv_ref, o_ref,
                     *, past, tq):
    heads = MLA_HEADS
    q_lat = jnp.concatenate(
        [jnp.dot(qn_ref[:, h * LANES:(h + 1) * LANES], wuk_ref[h], preferred_element_type=F32)
         for h in range(heads)], axis=0).astype(BF16)
    q_r = jnp.concatenate([qr_ref[:, h * LANES:(h + 1) * LANES] for h in range(heads)], axis=0)
    c_c = cc_ref[0, 0].astype(BF16)
    kr_c = krc_ref[0, 0].astype(BF16)
    c_n = cn_ref[...]
    s_c = _qk(q_lat, c_c) + _qk(q_r[:, :MLA_ROPE], kr_c)
    s_n = _qk(q_lat, c_n) + _qk(q_r, krn_ref[...])
    vis_c = (np.arange(past)[None, :] // CHUNK) <= ((past + np.arange(tq))[:, None] // CHUNK)
    vis_n = ((past + np.arange(tq))[None, :] // CHUNK) <= ((past + np.arange(tq))[:, None] // CHUNK)
    if not vis_c.all():
        frame = lax.broadcasted_iota(jnp.int32, s_c.shape, 0) % tq
        key = lax.broadcasted_iota(jnp.int32, s_c.shape, 1)
        s_c = jnp.where((key // CHUNK) <= ((past + frame) // CHUNK), s_c, NEG_BIG)
    if not vis_n.all():
        frame = lax.broadcasted_iota(jnp.int32, s_n.shape, 0) % tq
        key = lax.broadcasted_iota(jnp.int32, s_n.shape, 1)
        s_n = jnp.where(((past + key) // CHUNK) <= ((past + frame) // CHUNK), s_n, NEG_BIG)
    m = jnp.maximum(jnp.max(s_c, axis=1, keepdims=True), jnp.max(s_n, axis=1, keepdims=True))
    p_c = jnp.exp2(s_c - m)
    p_n = jnp.exp2(s_n - m)
    l = jnp.sum(p_c, axis=1, keepdims=True) + jnp.sum(p_n, axis=1, keepdims=True)
    o_lat = (jnp.dot(p_c.astype(BF16), c_c, preferred_element_type=F32)
             + jnp.dot(p_n.astype(BF16), c_n, preferred_element_type=F32)) / l
    o_lat = o_lat.astype(BF16)
    for h in range(heads):
        o_ref[:, h * LANES:(h + 1) * LANES] = jnp.dot(
            o_lat[h * tq:(h + 1) * tq], wuv_ref[h], preferred_element_type=F32).astype(o_ref.dtype)


def _mla_attn_sample(qn, qr, ckv_new, kr_new, cache_ckv, cache_kr, layer, w_uk_t, w_uv, *, row0, batch, tq):
    w = MLA_HEADS * LANES
    past = cache_ckv.shape[2]
    rb0 = row0 // tq
    return pl.pallas_call(
        functools.partial(_mla_sample_body, past=past, tq=tq),
        grid=(batch,),
        in_specs=[pl.BlockSpec((tq, w), lambda b: (rb0 + b, 0)),
                  pl.BlockSpec((tq, w), lambda b: (rb0 + b, 0)),
                  pl.BlockSpec((tq, MLA_KV_LORA), lambda b: (rb0 + b, 0)),
                  pl.BlockSpec((tq, LANES), lambda b: (rb0 + b, 0)),
                  pl.BlockSpec((1, 1, past, MLA_KV_LORA), lambda b: (layer, b, 0, 0)),
                  pl.BlockSpec((1, 1, past, MLA_ROPE), lambda b: (layer, b, 0, 0)),
                  pl.BlockSpec((MLA_HEADS, MLA_NOPE, MLA_KV_LORA), lambda b: (0, 0, 0)),
                  pl.BlockSpec((MLA_HEADS, MLA_KV_LORA, MLA_V), lambda b: (0, 0, 0))],
        out_specs=pl.BlockSpec((tq, w), lambda b: (b, 0)),
        out_shape=jax.ShapeDtypeStruct((batch * tq, w), BF16),
        compiler_params=_cparams(("parallel",)),
        name="mla_attn_sample",
    )(qn, qr, ckv_new, kr_new, cache_ckv, cache_kr, w_uk_t, w_uv)


def _cross_body(q_ref, k_ref, v_ref, o_ref, *, groups, tq, n_mem, interleaved):
    for g in range(groups):
        rows = slice(g * tq, (g + 1) * tq)
        for h in range(MEM_HEADS):
            sl = slice(h * HEAD_DIM, (h + 1) * HEAD_DIM)
            if interleaved:
                k = k_ref[g, pl.ds(h, n_mem, stride=MEM_HEADS), :]
                v = v_ref[g, pl.ds(h, n_mem, stride=MEM_HEADS), :]
            else:
                k, v = k_ref[g, :, sl], v_ref[g, :, sl]
            s = _qk(q_ref[rows, sl], k.astype(BF16))
            p = jnp.exp2(s - jnp.max(s, axis=1, keepdims=True))
            l = jnp.sum(p, axis=1, keepdims=True)
            o = jnp.dot(p.astype(BF16), v.astype(BF16), preferred_element_type=F32)
            o_ref[rows, sl] = (o / l).astype(o_ref.dtype)


def _cross_attn(q, mem_k, mem_v, *, layer=None, row0, n_rows, rows_per_stream, groups, tq):
    wq = MEM_HEADS * HEAD_DIM
    step_rows = groups * tq
    stream_block = lambda i: i // (rows_per_stream // tq)
    rb0 = row0 // step_rows
    interleaved = mem_v is not None
    if interleaved:
        n_layers, n_streams, n_mem = mem_k.shape[:3]
        flat = (n_layers, n_streams, n_mem * MEM_HEADS, HEAD_DIM)
        spec = pl.BlockSpec((None, groups, n_mem * MEM_HEADS, HEAD_DIM), lambda i: (layer, stream_block(i), 0, 0))
        mem_specs, mem_args = [spec, spec], [mem_k.reshape(flat), mem_v.reshape(flat)]
    else:
        n_mem = mem_k.shape[1]
        mem_specs = [pl.BlockSpec((groups, n_mem, wq), lambda i: (stream_block(i), 0, 0)),
                     pl.BlockSpec((groups, n_mem, wq), lambda i: (stream_block(i), 0, 1))]
        mem_args = [mem_k, mem_k]
    return pl.pallas_call(
        functools.partial(_cross_body, groups=groups, tq=tq, n_mem=n_mem, interleaved=interleaved),
        grid=(n_rows // step_rows,),
        in_specs=[pl.BlockSpec((step_rows, wq), lambda i: (rb0 + i, 0))] + mem_specs,
        out_specs=pl.BlockSpec((step_rows, wq), lambda i: (i, 0)),
        out_shape=jax.ShapeDtypeStruct((n_rows, wq), BF16),
        compiler_params=_cparams(("parallel",)),
        name="cross_attn",
    )(q, *mem_args)


def _split_bf16(x):
    hi = x.astype(BF16)
    return hi, (x - hi.astype(F32)).astype(BF16)


def _router_body(x_ref, g_ref, w_ref, h_ref, idx_ref, gate_ref):
    h = _rms(x_ref[...], g_ref[...])
    h_ref[...] = h
    h_hi, h_lo = _split_bf16(h)
    w_hi, w_lo = _split_bf16(w_ref[...])
    logits = (jnp.dot(h_hi, w_hi, preferred_element_type=F32)
              + jnp.dot(h_hi, w_lo, preferred_element_type=F32)
              + jnp.dot(h_lo, w_hi, preferred_element_type=F32))
    lane = lax.broadcasted_iota(jnp.int32, logits.shape, 1)
    logits = jnp.where(lane < N_EXPERTS, logits, NEG_BIG)
    v1 = jnp.max(logits, axis=1, keepdims=True)
    i1 = jnp.min(jnp.where(logits == v1, lane, LANES), axis=1, keepdims=True)
    rest = jnp.where(lane == i1, NEG_BIG, logits)
    v2 = jnp.max(rest, axis=1, keepdims=True)
    i2 = jnp.min(jnp.where(rest == v2, lane, LANES), axis=1, keepdims=True)
    e2 = jnp.exp(v2 - v1)
    g1 = 1.0 / (1.0 + e2)
    g2 = e2 / (1.0 + e2)
    idx_ref[...] = jnp.where(lane == 0, i1, jnp.where(lane == 1, i2, 0))
    gate_ref[...] = jnp.where(lane == 0, g1, jnp.where(lane == 1, g2, 0.0))


def _router(x, gain, w_router, *, tm=512):
    n, d = x.shape
    assert n % tm == 0
    return pl.pallas_call(
        _router_body,
        grid=(n // tm,),
        in_specs=[pl.BlockSpec((tm, d), lambda i: (i, 0)),
                  pl.BlockSpec((1, d), lambda i: (0, 0)),
                  pl.BlockSpec((d, LANES), lambda i: (0, 0))],
        out_specs=[pl.BlockSpec((tm, d), lambda i: (i, 0)),
                   pl.BlockSpec((tm, LANES), lambda i: (i, 0)),
                   pl.BlockSpec((tm, LANES), lambda i: (i, 0))],
        out_shape=[jax.ShapeDtypeStruct((n, d), F32),
                   jax.ShapeDtypeStruct((n, LANES), jnp.int32),
                   jax.ShapeDtypeStruct((n, LANES), F32)],
        compiler_params=_cparams(("parallel",)),
        name="router",
    )(x, gain.reshape(1, d).astype(F32), _pad_lanes(w_router))


ROW_DMA_UNROLL = 8


def _gather_body(idx_ref, src_ref, out_ref, buf_ref, sem, *, rows, n_steps):
    i = pl.program_id(0)

    def issue_block(blk):
        slot = blk % 2
        base = blk * rows

        def issue(g, carry):
            for u in range(ROW_DMA_UNROLL):
                r = g * ROW_DMA_UNROLL + u
                pltpu.make_async_copy(src_ref.at[pl.ds(idx_ref[base + r], 1)],
                                      buf_ref.at[slot, pl.ds(r, 1)], sem.at[slot]).start()
            return carry

        lax.fori_loop(0, rows // ROW_DMA_UNROLL, issue, 0)

    @pl.when(i == 0)
    def _():
        issue_block(i)

    @pl.when(i + 1 < n_steps)
    def _():
        issue_block(i + 1)

    slot = i % 2
    pltpu.make_async_copy(src_ref.at[pl.ds(0, rows)], buf_ref.at[slot], sem.at[slot]).wait()
    out_ref[...] = buf_ref[slot].astype(out_ref.dtype)


def _gather_rows(src, row_idx, *, out_dtype, rows=512):
    r, d = row_idx.shape[0], src.shape[1]
    assert r % rows == 0 and rows % ROW_DMA_UNROLL == 0 and src.dtype == F32
    return pl.pallas_call(
        functools.partial(_gather_body, rows=rows, n_steps=r // rows),
        grid_spec=pltpu.PrefetchScalarGridSpec(
            num_scalar_prefetch=1, grid=(r // rows,),
            in_specs=[pl.BlockSpec(memory_space=pl.ANY)],
            out_specs=pl.BlockSpec((rows, d), lambda i, idx: (i, 0)),
            scratch_shapes=[pltpu.VMEM((2, rows, d), F32), pltpu.SemaphoreType.DMA((2,))]),
        out_shape=jax.ShapeDtypeStruct((r, d), out_dtype),
        compiler_params=_cparams(("arbitrary",)),
        name="gather_rows",
    )(row_idx, src)


def _moe_body(wt_ref, we_ref, ws_ref, wc_ref, wf_ref, x_ref, wg_ref, wu_ref, wd_ref, out_ref, *, ts, nsub):
    w, f = pl.program_id(0), pl.program_id(1)

    @pl.when((f == 0) & (wf_ref[w] == 1))
    def _():
        out_ref[...] = jnp.zeros(out_ref.shape, F32)

    s0, cnt = ws_ref[w], wc_ref[w]
    for a in range(nsub):
        for c in range(1, nsub - a + 1):
            @pl.when((s0 == a) & (cnt == c))
            def _():
                rows = slice(a * ts, (a + c) * ts)
                out_ref[rows, :] += _swiglu_partial(x_ref[rows, :], wg_ref[...], wu_ref[...], wd_ref[...])


def _moe_ffn(xs, w_gu, w_d, layer, items, *, tm, ts, tf=512):
    r, d = xs.shape
    f = w_d.shape[2]
    nf = f // tf
    n_items = items[0].shape[0]
    assert r % tm == 0 and tm % ts == 0 and f % tf == 0

    def fidx(w, j, wc):
        return jnp.where(wc[w] > 0, j, nf - 1)

    return pl.pallas_call(
        functools.partial(_moe_body, ts=ts, nsub=tm // ts),
        grid_spec=pltpu.PrefetchScalarGridSpec(
            num_scalar_prefetch=5, grid=(n_items, nf),
            in_specs=[pl.BlockSpec((tm, d), lambda w, j, wt, we, ws, wc, wf: (wt[w], 0)),
                      pl.BlockSpec((None, None, d, tf),
                                   lambda w, j, wt, we, ws, wc, wf: (layer, we[w], 0, fidx(w, j, wc))),
                      pl.BlockSpec((None, None, d, tf),
                                   lambda w, j, wt, we, ws, wc, wf: (layer, we[w], 0, nf + fidx(w, j, wc))),
                      pl.BlockSpec((None, None, tf, d),
                                   lambda w, j, wt, we, ws, wc, wf: (layer, we[w], fidx(w, j, wc), 0))],
            out_specs=pl.BlockSpec((tm, d), lambda w, j, wt, we, ws, wc, wf: (wt[w], 0))),
        out_shape=jax.ShapeDtypeStruct((r, d), F32),
        compiler_params=_cparams(("arbitrary", "arbitrary")),
        name="moe_ffn",
    )(*items, xs, w_gu, w_gu, w_d)


def _combine_body(p0_ref, p1_ref, x_ref, gate_ref, ys_ref, out_ref, buf_ref, sem, *, rows, n_steps):
    i = pl.program_id(0)

    def issue_block(blk):
        slot = blk % 2
        base = blk * rows

        def issue(g, carry):
            for u in range(ROW_DMA_UNROLL):
                r = g * ROW_DMA_UNROLL + u
                pltpu.make_async_copy(ys_ref.at[pl.ds(p0_ref[base + r], 1)],
                                      buf_ref.at[slot, 0, pl.ds(r, 1)], sem.at[slot]).start()
                pltpu.make_async_copy(ys_ref.at[pl.ds(p1_ref[base + r], 1)],
                                      buf_ref.at[slot, 1, pl.ds(r, 1)], sem.at[slot]).start()
            return carry

        lax.fori_loop(0, rows // ROW_DMA_UNROLL, issue, 0)

    @pl.when(i == 0)
    def _():
        issue_block(i)

    @pl.when(i + 1 < n_steps)
    def _():
        issue_block(i + 1)

    slot = i % 2
    pltpu.make_async_copy(ys_ref.at[pl.ds(0, rows)], buf_ref.at[slot, 0], sem.at[slot]).wait()
    pltpu.make_async_copy(ys_ref.at[pl.ds(0, rows)], buf_ref.at[slot, 1], sem.at[slot]).wait()
    g = gate_ref[...]
    out_ref[...] = x_ref[...] + (g[:, 0:1] * buf_ref[slot, 0] + g[:, 1:2] * buf_ref[slot, 1])


def _moe_combine(x, gates, ys, pos0, pos1, *, rows=256):
    n, d = x.shape
    assert n % rows == 0 and rows % ROW_DMA_UNROLL == 0
    return pl.pallas_call(
        functools.partial(_combine_body, rows=rows, n_steps=n // rows),
        grid_spec=pltpu.PrefetchScalarGridSpec(
            num_scalar_prefetch=2, grid=(n // rows,),
            in_specs=[pl.BlockSpec((rows, d), lambda i, p0, p1: (i, 0)),
                      pl.BlockSpec((rows, LANES), lambda i, p0, p1: (i, 0)),
                      pl.BlockSpec(memory_space=pl.ANY)],
            out_specs=pl.BlockSpec((rows, d), lambda i, p0, p1: (i, 0)),
            scratch_shapes=[pltpu.VMEM((2, 2, rows, d), F32), pltpu.SemaphoreType.DMA((2,))]),
        out_shape=jax.ShapeDtypeStruct((n, d), F32),
        compiler_params=_cparams(("arbitrary",)),
        name="moe_combine",
    )(pos0, pos1, x, gates, ys)


def _moe_plan(idx, *, tm, ts):
    n = idx.shape[0]
    nsub = tm // ts
    cap = -(-(2 * n + N_EXPERTS * ts) // tm) * tm
    n_tiles, n_subs = cap // tm, cap // ts
    n_items = n_tiles + N_EXPERTS - 1
    e_flat = idx.reshape(-1)
    onehot = (e_flat[:, None] == jnp.arange(N_EXPERTS, dtype=jnp.int32)[None, :]).astype(jnp.int32)
    before = jnp.cumsum(onehot, axis=0) - onehot
    rank = jnp.sum(before * onehot, axis=1)
    counts = jnp.sum(onehot, axis=0)
    padded = ((counts + ts - 1) // ts) * ts
    g_end = jnp.cumsum(padded)
    g_start = g_end - padded
    pos = g_start[e_flat] + rank
    row_token = jnp.zeros((cap,), jnp.int32).at[pos].set(jnp.arange(2 * n, dtype=jnp.int32) // 2)
    s = jnp.arange(n_subs, dtype=jnp.int32)
    sub_e = jnp.sum((s[:, None] * ts >= g_end[None, :]).astype(jnp.int32), axis=1)
    valid = sub_e < N_EXPERTS
    prev_e = jnp.concatenate([jnp.full((1,), -1, jnp.int32), sub_e[:-1]])
    tile_start = (s % nsub) == 0
    new = tile_start | (valid & (sub_e != prev_e))
    item_of_sub = jnp.cumsum(new.astype(jnp.int32)) - 1
    n_used = item_of_sub[-1] + 1
    tgt = jnp.where(new, item_of_sub, n_items)
    w = jnp.arange(n_items, dtype=jnp.int32)
    used = w < n_used
    it_tile = jnp.full((n_items,), n_tiles - 1, jnp.int32).at[tgt].set(s // nsub, mode="drop")
    it_exp = jnp.full((n_items,), N_EXPERTS - 1, jnp.int32).at[tgt].set(
        jnp.minimum(sub_e, N_EXPERTS - 1), mode="drop")
    it_s0 = jnp.zeros((n_items,), jnp.int32).at[tgt].set(s % nsub, mode="drop")
    it_first = jnp.zeros((n_items,), jnp.int32).at[tgt].set(tile_start.astype(jnp.int32), mode="drop")
    it_cnt = jnp.zeros((n_items,), jnp.int32).at[item_of_sub].add(valid.astype(jnp.int32), mode="drop")
    it_cnt = jnp.where(used, it_cnt, 0)
    return pos[0::2], pos[1::2], row_token, (it_tile, it_exp, it_s0, it_cnt, it_first)


def _moe_layer(x, gain, w_router, w_gu, w_d, layer, *, tm=1024, ts=256):
    h, idx, gates = _router(x, gain, w_router)
    pos0, pos1, row_token, items = _moe_plan(idx[:, :2], tm=tm, ts=ts)
    xs = _gather_rows(h, row_token, out_dtype=BF16)
    ys = _moe_ffn(xs, w_gu, w_d, layer, items, tm=tm, ts=ts)
    return _moe_combine(x, gates, ys, pos0, pos1)


def _fox_layer(x, gain, w_in, b_f, w_o, cache_k, cache_v, cache_logf, layer, *, bp, tp, bs, tq):
    n_p = bp * tp
    w = FOX_HEADS * HEAD_DIM
    q_scale = jnp.full((w,), HEAD_DIM ** -0.5 * LOG2E, F32)
    q = _matmul(x, w_in, layer=layer, gain=gain, col_scale=q_scale, m=w, out_dtypes=(BF16,))
    kv, kv_b = _matmul(x, w_in, layer=layer, gain=gain, w_col0=w, m=2 * w, out_dtypes=(F32, BF16))
    logf = _matmul(x, _pad_lanes(w_in[layer, :, 3 * w:]), gain=gain, bias=_pad_lanes(b_f),
                   act="log_sigmoid")[:, :FOX_HEADS]
    logf_p = logf[:n_p].reshape(bp, tp, FOX_HEADS)
    logf_s = logf[n_p:].reshape(bs, tq, FOX_HEADS)
    past = cache_logf.shape[2]
    cs_p = _cumsum_time(logf_p)
    cs_s = _cumsum_time(jnp.concatenate([cache_logf[layer], logf_s], axis=1))
    aq, ak = _fox_aug(cs_p.reshape(n_p, FOX_HEADS))
    o_p = _fox_attn_prompt(q, kv_b, aq, ak, batch=bp, seq=tp)
    o_s = _fox_attn_sample(q, kv_b, cache_k, cache_v, layer, cs_s[:, past:], cs_s.transpose(0, 2, 1),
                           row0=n_p, batch=bs, tq=tq)
    x = _matmul((o_p, o_s), w_o, layer=layer, res=x)
    return x, kv[:, :w], kv[:, w:], logf


def _mla_layer(x, gain, w_in, g_q, g_kv, w_uq, w_ukv, w_o, layer, cache_ckv, cache_kr, cos, sin,
               *, bp, tp, bs, tq):
    n_p = bp * tp
    h, c, r = MLA_HEADS, MLA_KV_LORA, MLA_ROPE
    w_kr = w_in[:, MLA_Q_LORA + c:]
    w_in_big = jnp.concatenate([w_in[:, :MLA_Q_LORA + c], _pad_lanes(w_kr), _pad_lanes(_swap_halves(w_kr))], axis=1)
    w_q = w_uq.reshape(MLA_Q_LORA, h, MLA_NOPE + r)
    w_qr = w_q[:, :, MLA_NOPE:]
    w_uq_big = jnp.concatenate([w_q[:, :, :MLA_NOPE].reshape(MLA_Q_LORA, h * MLA_NOPE),
                                _pad_lanes(w_qr).reshape(MLA_Q_LORA, h * LANES),
                                _pad_lanes(_swap_halves(w_qr)).reshape(MLA_Q_LORA, h * LANES)], axis=1)
    w_kv = w_ukv.reshape(c, h, MLA_NOPE + MLA_V)
    w_ukv_split = jnp.concatenate([w_kv[:, :, :MLA_NOPE].reshape(c, h * MLA_NOPE),
                                   w_kv[:, :, MLA_NOPE:].reshape(c, h * MLA_V)], axis=1)
    w_uk_t = w_kv[:, :, :MLA_NOPE].transpose(1, 2, 0).astype(BF16)
    w_uv = w_kv[:, :, MLA_NOPE:].transpose(1, 0, 2).astype(BF16)
    q_scale = (MLA_NOPE + r) ** -0.5 * LOG2E

    a = _matmul(x, w_in_big, gain=gain, tn=256)
    ckv, ckv_b = _rmsnorm(a, g_kv, x_col=1, k=c, out_dtypes=(F32, BF16))
    kr, kr_b = _rope(a, cos, sin, x_col=(MLA_Q_LORA + c) // (2 * LANES), width=LANES, out_dtypes=(F32, BF16))
    qn = _matmul(a, w_uq_big, gain=g_q, x_col=0, k=MLA_Q_LORA, m=h * MLA_NOPE,
                 col_scale=jnp.full((h * MLA_NOPE,), q_scale, F32), out_dtypes=(BF16,))
    qr2 = _matmul(a, w_uq_big, gain=g_q, x_col=0, k=MLA_Q_LORA, w_col0=h * MLA_NOPE,
                  col_scale=jnp.full((2 * h * LANES,), q_scale, F32))
    qr = _rope(qr2, cos, sin, width=h * LANES, out_dtypes=(BF16,))
    kv = _matmul(ckv_b, w_ukv_split, n_rows=n_p, out_dtypes=(BF16,))
    o_p = _mla_attn_prompt(qn, qr, kv, kr_b, batch=bp, seq=tp)
    o_s = _mla_attn_sample(qn, qr, ckv_b, kr_b, cache_ckv, cache_kr, layer, w_uk_t, w_uv,
                           row0=n_p, batch=bs, tq=tq)
    x = _matmul((o_p, o_s), w_o, layer=layer, res=x)
    return x, ckv, kr[:, :r]


def _cross_layer(x, gain, w_q, w_o, layer, kv_p, cache_k, cache_v, *, bp, tp, bs, tq):
    n_p = bp * tp
    wq = MEM_HEADS * HEAD_DIM
    q = _matmul(x, w_q, layer=layer, gain=gain, col_scale=jnp.full((wq,), HEAD_DIM ** -0.5 * LOG2E, F32),
                out_dtypes=(BF16,))
    o_p = _cross_attn(q, kv_p, None, row0=0, n_rows=n_p, rows_per_stream=tp, groups=1, tq=1024)
    o_s = _cross_attn(q, cache_k, cache_v, layer=layer, row0=n_p, n_rows=bs * tq, rows_per_stream=tq,
                      groups=8, tq=tq)
    return _matmul((o_p, o_s), w_o, layer=layer, res=x)


def kernel(x_prompt, x_sample, cache_fox_k, cache_fox_v, cache_fox_logf, cache_mla_ckv, cache_mla_krope, cache_mem_k, cache_mem_v, mem_prompt, norm_mix, norm_cross, norm_ffn, norm_final, fox_w_in, fox_b_f, fox_w_o, mla_w_in, mla_g_q, mla_g_kv, mla_w_uq, mla_w_ukv, mla_w_o, cross_w_q, cross_w_kv, cross_w_o, ffn_w_gu, ffn_w_d, moe_w_router, moe_w_gu, moe_w_d):
    bp, tp, d = x_prompt.shape
    bs, tq, _ = x_sample.shape
    n_p = bp * tp
    depth = norm_mix.shape[0]
    past = cache_fox_k.shape[2]
    n_mem = mem_prompt.shape[1]
    mem_w = MEM_HEADS * HEAD_DIM
    dims = dict(bp=bp, tp=tp, bs=bs, tq=tq)

    x = jnp.concatenate([x_prompt.reshape(n_p, d), x_sample.reshape(bs * tq, d)], axis=0)
    pos = np.concatenate([np.tile(np.arange(tp), bp), np.tile(past + np.arange(tq), bs)])
    cos, sin = _rope_tables(pos)
    mem_flat = mem_prompt.reshape(bp * n_mem, d)

    fox_rows, mla_rows, mem_rows = [], [], []
    for i in range(depth):
        j = i // 2
        if i % 2 == 0:
            x, k, v, lf = _fox_layer(x, norm_mix[i], fox_w_in, fox_b_f[j], fox_w_o,
                                     cache_fox_k, cache_fox_v, cache_fox_logf, j, **dims)
            fox_rows.append((k, v, lf))
        else:
            x, ckv, kr = _mla_layer(x, norm_mix[i], mla_w_in[j], mla_g_q[j], mla_g_kv[j], mla_w_uq[j],
                                    mla_w_ukv[j], mla_w_o, j, cache_mla_ckv, cache_mla_krope,
                                    cos, sin, **dims)
            mla_rows.append((ckv, kr))
        kv_p = _matmul(mem_flat, cross_w_kv, layer=i).reshape(bp, n_mem, 2 * mem_w)
        mem_rows.append((kv_p[:, :, :mem_w], kv_p[:, :, mem_w:]))
        x = _cross_layer(x, norm_cross[i], cross_w_q, cross_w_o, i, kv_p, cache_mem_k, cache_mem_v, **dims)
        if i % 2 == 0:
            x = _ffn(x, norm_ffn[i], ffn_w_gu, ffn_w_d, j)
        else:
            x = _moe_layer(x, norm_ffn[i], moe_w_router[j], moe_w_gu, moe_w_d, j)
    y_p = _rmsnorm(x, norm_final, n_rows=n_p).reshape(bp, tp, d)
    y_s = _rmsnorm(x, norm_final, row0=n_p).reshape(bs, tq, d)

    def split(rows, tail):
        return rows[:n_p].reshape((bp, tp) + tail), rows[n_p:].reshape((bs, tq) + tail)

    def stacked(rows_list, tail):
        parts = [split(r, tail) for r in rows_list]
        return jnp.stack([p[0] for p in parts]), jnp.stack([p[1] for p in parts])

    hd = (FOX_HEADS, HEAD_DIM)
    fox_k_p, fox_k_s = stacked([r[0] for r in fox_rows], hd)
    fox_v_p, fox_v_s = stacked([r[1] for r in fox_rows], hd)
    fox_lf_p, fox_lf_s = stacked([r[2] for r in fox_rows], (FOX_HEADS,))
    mla_ckv_p, mla_ckv_s = stacked([r[0] for r in mla_rows], (MLA_KV_LORA,))
    mla_kr_p, mla_kr_s = stacked([r[1] for r in mla_rows], (MLA_ROPE,))
    mem_k_p = jnp.stack([m[0].reshape(bp, n_mem, MEM_HEADS, HEAD_DIM) for m in mem_rows])
    mem_v_p = jnp.stack([m[1].reshape(bp, n_mem, MEM_HEADS, HEAD_DIM) for m in mem_rows])
    return (y_p, y_s, fox_k_p, fox_v_p, fox_lf_p, mla_ckv_p, mla_kr_p, mem_k_p, mem_v_p,
            fox_k_s, fox_v_s, fox_lf_s, mla_ckv_s, mla_kr_s)
```

```python
import functools

import numpy as np
import jax
import jax.numpy as jnp
from jax import lax
from jax.experimental import pallas as pl
from jax.experimental.pallas import tpu as pltpu

F32 = jnp.float32
BF16 = jnp.bfloat16

D_MODEL = 2048
CHUNK = 64
RMS_EPS = 1e-6
FOX_HEADS = 16
HEAD_DIM = 128
MLA_HEADS = 16
MLA_Q_LORA = 512
MLA_KV_LORA = 512
MLA_NOPE = 128
MLA_ROPE = 64
MLA_V = 128
ROPE_BASE = 10000.0
MEM_HEADS = 4
N_EXPERTS = 8
LANES = 128

VMEM_LIMIT_BYTES = 56 * 1024 * 1024
NEG_BIG = -1e30
LOG2E = 1.4426950408889634


def _cparams(sem):
    return pltpu.CompilerParams(dimension_semantics=sem, vmem_limit_bytes=VMEM_LIMIT_BYTES)


def _rms(xf, g):
    return xf * lax.rsqrt(jnp.mean(xf * xf, axis=-1, keepdims=True) + RMS_EPS) * g


def _log_sigmoid(x):
    return jnp.minimum(x, 0.0) - jnp.log1p(jnp.exp(-jnp.abs(x)))


def _split3_bf16(x):
    hi = x.astype(BF16)
    r1 = x - hi.astype(F32)
    mid = r1.astype(BF16)
    lo = (r1 - mid.astype(F32)).astype(BF16)
    return hi, mid, lo


def _mm_body(*refs, has_norm, has_bias, has_scale, has_res, act, use_scratch, n_out, split):
    it = iter(refs)
    x_ref = next(it)
    x2_ref = next(it) if split is not None else None
    g_ref = next(it) if has_norm else None
    w_ref = next(it)
    b_ref = next(it) if has_bias else None
    s_ref = next(it) if has_scale else None
    r_ref = next(it) if has_res else None
    out_refs = [next(it) for _ in range(n_out)]
    xb_ref = next(it) if use_scratch else None

    def stage(src_ref):
        xf = src_ref[...].astype(F32)
        if has_norm:
            xf = _rms(xf, g_ref[...])
        xb_ref[...] = xf.astype(BF16)

    if use_scratch:
        first = pl.program_id(1) == 0
        if split is None:
            pl.when(first)(lambda: stage(x_ref))
        else:
            pl.when(first & (pl.program_id(0) < split))(lambda: stage(x_ref))
            pl.when(first & (pl.program_id(0) >= split))(lambda: stage(x2_ref))
        xb = xb_ref[...]
    else:
        xb = x_ref[...]
    acc = jnp.dot(xb, w_ref[...].astype(BF16), preferred_element_type=F32)
    if has_bias:
        acc = acc + b_ref[...]
    if act == "log_sigmoid":
        acc = _log_sigmoid(acc)
    if has_scale:
        acc = acc * s_ref[...]
    if has_res:
        acc = acc + r_ref[...]
    for o in out_refs:
        o[...] = acc.astype(o.dtype)


def _matmul(x, w, *, layer=None, gain=None, bias=None, col_scale=None, res=None, act=None, x_col=0,
            k=None, w_col0=0, m=None, n_rows=None, out_dtypes=(F32,), tm=1024, tn=1024):
    x, x2 = x if isinstance(x, tuple) else (x, None)
    n = (x.shape[0] + (0 if x2 is None else x2.shape[0])) if n_rows is None else n_rows
    k = x.shape[1] if k is None else k
    m = w.shape[-1] - w_col0 if m is None else m
    tm = min(tm, n)
    tn = next(c for c in (tn, 512, 256, 128, m) if m % c == 0 and w_col0 % c == 0)
    assert n % tm == 0 and w.shape[-2] == k
    assert (w.ndim == 3) == (layer is not None)
    has_norm, has_bias, has_res = gain is not None, bias is not None, res is not None
    has_scale = col_scale is not None
    use_scratch = has_norm or x.dtype != BF16 or x2 is not None
    col0 = w_col0 // tn
    row_vec = pl.BlockSpec((1, tn), lambda i, j: (0, j))
    if x2 is None:
        split = None
        in_specs = [pl.BlockSpec((tm, k), lambda i, j: (i, x_col))]
        args = [x]
    else:
        assert x.shape[0] % tm == 0 and x2.shape[0] % tm == 0 and n_rows is None
        split = x.shape[0] // tm
        in_specs = [pl.BlockSpec((tm, k), lambda i, j: (jnp.minimum(i, split - 1), x_col)),
                    pl.BlockSpec((tm, k), lambda i, j: (jnp.maximum(i - split, 0), x_col))]
        args = [x, x2]
    if has_norm:
        in_specs.append(pl.BlockSpec((1, k), lambda i, j: (0, 0)))
        args.append(gain.reshape(1, k).astype(F32))
    if layer is None:
        in_specs.append(pl.BlockSpec((k, tn), lambda i, j: (0, j + col0)))
    else:
        in_specs.append(pl.BlockSpec((None, k, tn), lambda i, j: (layer, 0, j + col0)))
    args.append(w)
    if has_bias:
        in_specs.append(row_vec)
        args.append(bias.reshape(1, m).astype(F32))
    if has_scale:
        in_specs.append(row_vec)
        args.append(col_scale.reshape(1, m).astype(F32))
    if has_res:
        in_specs.append(pl.BlockSpec((tm, tn), lambda i, j: (i, j)))
        args.append(res)
    body = functools.partial(_mm_body, has_norm=has_norm, has_bias=has_bias, has_scale=has_scale,
                             has_res=has_res, act=act, use_scratch=use_scratch, n_out=len(out_dtypes),
                             split=split)
    outs = pl.pallas_call(
        body,
        grid=(n // tm, m // tn),
        in_specs=in_specs,
        out_specs=[pl.BlockSpec((tm, tn), lambda i, j: (i, j)) for _ in out_dtypes],
        out_shape=[jax.ShapeDtypeStruct((n, m), dt) for dt in out_dtypes],
        scratch_shapes=[pltpu.VMEM((tm, k), BF16)] if use_scratch else [],
        compiler_params=_cparams(("parallel", "arbitrary")),
        name="matmul",
    )(*args)
    return outs[0] if len(out_dtypes) == 1 else outs


def _norm_body(x_ref, g_ref, *out_refs):
    y = _rms(x_ref[...].astype(F32), g_ref[...])
    for o in out_refs:
        o[...] = y.astype(o.dtype)


def _rmsnorm(x, gain, *, x_col=0, k=None, row0=0, n_rows=None, out_dtypes=(F32,), tm=1024):
    n = x.shape[0] - row0 if n_rows is None else n_rows
    k = x.shape[1] if k is None else k
    tm = min(tm, n)
    assert n % tm == 0 and row0 % tm == 0
    rb0 = row0 // tm
    outs = pl.pallas_call(
        _norm_body,
        grid=(n // tm,),
        in_specs=[pl.BlockSpec((tm, k), lambda i: (rb0 + i, x_col)),
                  pl.BlockSpec((1, k), lambda i: (0, 0))],
        out_specs=[pl.BlockSpec((tm, k), lambda i: (i, 0)) for _ in out_dtypes],
        out_shape=[jax.ShapeDtypeStruct((n, k), dt) for dt in out_dtypes],
        compiler_params=_cparams(("parallel",)),
        name="rmsnorm",
    )(x, gain.reshape(1, k).astype(F32))
    return outs[0] if len(out_dtypes) == 1 else outs


def _swiglu_partial(xb, wg, wu, wd):
    g = jnp.dot(xb, wg.astype(BF16), preferred_element_type=F32)
    u = jnp.dot(xb, wu.astype(BF16), preferred_element_type=F32)
    a = (g * jax.nn.sigmoid(g)) * u
    return jnp.dot(a.astype(BF16), wd.astype(BF16), preferred_element_type=F32)


def _ffn_body(x_ref, g_ref, wg_ref, wu_ref, wd_ref, out_ref, xb_ref):
    @pl.when(pl.program_id(1) == 0)
    def _():
        xf = x_ref[...]
        xb_ref[...] = _rms(xf, g_ref[...]).astype(BF16)
        out_ref[...] = xf
    out_ref[...] += _swiglu_partial(xb_ref[...], wg_ref[...], wu_ref[...], wd_ref[...])


def _ffn(x, gain, w_gu, w_d, layer, *, tm=1024, tf=256):
    n, d = x.shape
    f = w_d.shape[1]
    assert n % tm == 0 and f % tf == 0
    nf = f // tf
    return pl.pallas_call(
        _ffn_body,
        grid=(n // tm, nf),
        in_specs=[pl.BlockSpec((tm, d), lambda i, j: (i, 0)),
                  pl.BlockSpec((1, d), lambda i, j: (0, 0)),
                  pl.BlockSpec((None, d, tf), lambda i, j: (layer, 0, j)),
                  pl.BlockSpec((None, d, tf), lambda i, j: (layer, 0, j + nf)),
                  pl.BlockSpec((None, tf, d), lambda i, j: (layer, j, 0))],
        out_specs=pl.BlockSpec((tm, d), lambda i, j: (i, 0)),
        out_shape=jax.ShapeDtypeStruct((n, d), F32),
        scratch_shapes=[pltpu.VMEM((tm, d), BF16)],
        compiler_params=_cparams(("parallel", "arbitrary")),
        name="ffn",
    )(x, gain.reshape(1, d).astype(F32), w_gu, w_gu, w_d)


def _cumsum_body(x_ref, out_ref, *, chunk):
    t = x_ref.shape[1]
    row = lax.broadcasted_iota(jnp.int32, (chunk, chunk), 0)
    col = lax.broadcasted_iota(jnp.int32, (chunk, chunk), 1)
    tri = jnp.where(col <= row, 1.0, 0.0).astype(BF16)
    carry = jnp.zeros((1, x_ref.shape[2]), F32)
    for c in range(t // chunk):
        hi, mid, lo = _split3_bf16(x_ref[0, c * chunk:(c + 1) * chunk, :])
        cs = (jnp.dot(tri, hi, preferred_element_type=F32)
              + jnp.dot(tri, mid, preferred_element_type=F32)
              + jnp.dot(tri, lo, preferred_element_type=F32)) + carry
        out_ref[0, c * chunk:(c + 1) * chunk, :] = cs
        carry = cs[chunk - 1:chunk, :]


def _cumsum_time(x):
    b, t, h = x.shape
    chunk = next(c for c in (256, 128, 64, 32, 16, 8) if t % c == 0)
    return pl.pallas_call(
        functools.partial(_cumsum_body, chunk=chunk),
        grid=(b,),
        in_specs=[pl.BlockSpec((1, t, h), lambda i: (i, 0, 0))],
        out_specs=pl.BlockSpec((1, t, h), lambda i: (i, 0, 0)),
        out_shape=jax.ShapeDtypeStruct((b, t, h), F32),
        compiler_params=_cparams(("parallel",)),
        name="cumsum",
    )(x)


def _qk(q, k):
    return lax.dot_general(q, k, (((1,), (1,)), ((), ())), preferred_element_type=F32)


def _flash_step(s, v, m_ref, l_ref, acc_ref, h, lanes):
    m_old = m_ref[h]
    m_new = jnp.maximum(m_old, jnp.max(s, axis=1, keepdims=True))
    alpha = jnp.exp2(m_old - m_new)
    p = jnp.exp2(s - jnp.concatenate([m_new] * (s.shape[1] // LANES), axis=1))
    l_ref[h] = alpha * l_ref[h] + jnp.sum(p, axis=1, keepdims=True)
    acc_ref[:, lanes] = alpha * acc_ref[:, lanes] + jnp.dot(p.astype(BF16), v, preferred_element_type=F32)
    m_ref[h] = m_new


def _flash_init(m_ref, l_ref, acc_ref):
    m_ref[...] = jnp.full(m_ref.shape, NEG_BIG, F32)
    l_ref[...] = jnp.zeros(l_ref.shape, F32)
    acc_ref[...] = jnp.zeros(acc_ref.shape, F32)


def _flash_finish(o_ref, l_ref, acc_ref, heads):
    for h in range(heads):
        sl = slice(h * LANES, (h + 1) * LANES)
        o_ref[:, sl] = (acc_ref[:, sl] / l_ref[h]).astype(o_ref.dtype)


def _flash_scratch(t, heads):
    return [pltpu.VMEM((heads, t, LANES), F32), pltpu.VMEM((heads, t, LANES), F32),
            pltpu.VMEM((t, heads * LANES), F32)]


def _fox_aug_body(cs_ref, aq_ref, ak_ref):
    c = cs_ref[...] * LOG2E
    shape = (c.shape[0], LANES)
    lane = lax.broadcasted_iota(jnp.int32, shape, 1)
    for h in range(FOX_HEADS):
        hi, mid, lo = (p.astype(F32) for p in _split3_bf16(jnp.broadcast_to(c[:, h:h + 1], shape)))
        aq = jnp.where(lane == 0, hi, jnp.where(lane == 1, mid, jnp.where(lane == 2, lo,
                       jnp.where(lane < 6, 1.0, 0.0))))
        ak = jnp.where(lane < 3, 1.0, jnp.where(lane == 3, -hi, jnp.where(lane == 4, -mid,
                       jnp.where(lane == 5, -lo, 0.0))))
        sl = slice(h * LANES, (h + 1) * LANES)
        aq_ref[:, sl] = aq.astype(BF16)
        ak_ref[:, sl] = ak.astype(BF16)


def _fox_aug(csum_rows, *, tm=512):
    n, h = csum_rows.shape
    assert n % tm == 0
    return pl.pallas_call(
        _fox_aug_body,
        grid=(n // tm,),
        in_specs=[pl.BlockSpec((tm, h), lambda i: (i, 0))],
        out_specs=[pl.BlockSpec((tm, h * LANES), lambda i: (i, 0))] * 2,
        out_shape=[jax.ShapeDtypeStruct((n, h * LANES), BF16)] * 2,
        compiler_params=_cparams(("parallel",)),
        name="fox_aug",
    )(csum_rows)


def _fox_prompt_body(q_ref, aq_ref, k_ref, v_ref, ak_ref, o_ref, m_ref, l_ref, acc_ref, *, t):
    qi, ki = pl.program_id(1), pl.program_id(2)

    @pl.when(ki == 0)
    def _():
        _flash_init(m_ref, l_ref, acc_ref)

    def block(diagonal):
        if diagonal:
            allowed = (lax.broadcasted_iota(jnp.int32, (t, t), 1)
                       <= lax.broadcasted_iota(jnp.int32, (t, t), 0))
        for h in range(FOX_HEADS):
            sl = slice(h * LANES, (h + 1) * LANES)
            q = jnp.concatenate([q_ref[:, sl], aq_ref[:, sl]], axis=1)
            k = jnp.concatenate([k_ref[:, sl], ak_ref[:, sl]], axis=1)
            s = _qk(q, k)
            if diagonal:
                s = jnp.where(allowed, s, NEG_BIG)
            _flash_step(s, v_ref[:, sl], m_ref, l_ref, acc_ref, h, sl)

    @pl.when(ki < qi)
    def _():
        block(False)

    @pl.when(ki == qi)
    def _():
        block(True)
        _flash_finish(o_ref, l_ref, acc_ref, FOX_HEADS)


def _fox_attn_prompt(q, kv, aq, ak, *, batch, seq, t=512):
    w = FOX_HEADS * HEAD_DIM
    nt = seq // t
    assert seq % t == 0
    q_row = lambda b, i, j: b * nt + i
    k_row = lambda b, i, j: b * nt + jnp.minimum(i, j)
    return pl.pallas_call(
        functools.partial(_fox_prompt_body, t=t),
        grid=(batch, nt, nt),
        in_specs=[pl.BlockSpec((t, w), lambda b, i, j: (q_row(b, i, j), 0)),
                  pl.BlockSpec((t, w), lambda b, i, j: (q_row(b, i, j), 0)),
                  pl.BlockSpec((t, w), lambda b, i, j: (k_row(b, i, j), 0)),
                  pl.BlockSpec((t, w), lambda b, i, j: (k_row(b, i, j), 1)),
                  pl.BlockSpec((t, w), lambda b, i, j: (k_row(b, i, j), 0))],
        out_specs=pl.BlockSpec((t, w), lambda b, i, j: (q_row(b, i, j), 0)),
        out_shape=jax.ShapeDtypeStruct((batch * seq, w), BF16),
        scratch_shapes=_flash_scratch(t, FOX_HEADS),
        compiler_params=_cparams(("parallel", "parallel", "arbitrary")),
        name="fox_attn_prompt",
    )(q, aq, kv, kv, ak)


def _fox_sample_body(q_ref, kn_ref, vn_ref, kc_ref, vc_ref, cq_ref, ck_ref, o_ref, *, past, tq):
    allowed = (lax.broadcasted_iota(jnp.int32, (tq, tq), 1)
               <= lax.broadcasted_iota(jnp.int32, (tq, tq), 0))
    for h in range(FOX_HEADS):
        sl = slice(h * HEAD_DIM, (h + 1) * HEAD_DIM)
        q = q_ref[:, sl]
        cq = cq_ref[0, :, h:h + 1] * LOG2E
        ck = ck_ref[0, h:h + 1, :] * LOG2E
        kc = kc_ref[0, 0, pl.ds(h, past, stride=FOX_HEADS), :].astype(BF16)
        vc = vc_ref[0, 0, pl.ds(h, past, stride=FOX_HEADS), :].astype(BF16)
        s_c = _qk(q, kc) + (cq - ck[:, :past])
        s_n = _qk(q, kn_ref[:, sl]) + (cq - ck[:, past:])
        s_n = jnp.where(allowed, s_n, NEG_BIG)
        m = jnp.maximum(jnp.max(s_c, axis=1, keepdims=True), jnp.max(s_n, axis=1, keepdims=True))
        p_c = jnp.exp2(s_c - m)
        p_n = jnp.exp2(s_n - m)
        l = jnp.sum(p_c, axis=1, keepdims=True) + jnp.sum(p_n, axis=1, keepdims=True)
        o = (jnp.dot(p_c.astype(BF16), vc, preferred_element_type=F32)
             + jnp.dot(p_n.astype(BF16), vn_ref[:, sl], preferred_element_type=F32))
        o_ref[:, sl] = (o / l).astype(o_ref.dtype)


def _fox_attn_sample(q, kv, cache_k, cache_v, layer, csum_q, csum_t, *, row0, batch, tq):
    w = FOX_HEADS * HEAD_DIM
    n_layers, n_streams, past = cache_k.shape[:3]
    rb0 = row0 // tq
    flat = (n_layers, n_streams, past * FOX_HEADS, HEAD_DIM)
    cache_spec = pl.BlockSpec((1, 1, past * FOX_HEADS, HEAD_DIM), lambda b: (layer, b, 0, 0))
    return pl.pallas_call(
        functools.partial(_fox_sample_body, past=past, tq=tq),
        grid=(batch,),
        in_specs=[pl.BlockSpec((tq, w), lambda b: (rb0 + b, 0)),
                  pl.BlockSpec((tq, w), lambda b: (rb0 + b, 0)),
                  pl.BlockSpec((tq, w), lambda b: (rb0 + b, 1)),
                  cache_spec, cache_spec,
                  pl.BlockSpec((1, tq, FOX_HEADS), lambda b: (b, 0, 0)),
                  pl.BlockSpec((1, FOX_HEADS, past + tq), lambda b: (b, 0, 0))],
        out_specs=pl.BlockSpec((tq, w), lambda b: (b, 0)),
        out_shape=jax.ShapeDtypeStruct((batch * tq, w), BF16),
        compiler_params=_cparams(("parallel",)),
        name="fox_attn_sample",
    )(q, kv, kv, cache_k.reshape(flat), cache_v.reshape(flat), csum_q, csum_t)


def _rope_body(x_ref, cos_ref, sin_ref, *out_refs, width):
    cos, sin = cos_ref[...], sin_ref[...]
    for h in range(width // LANES):
        a = x_ref[:, h * LANES:(h + 1) * LANES]
        b = x_ref[:, width + h * LANES:width + (h + 1) * LANES]
        y = a * cos + b * sin
        for o in out_refs:
            o[:, h * LANES:(h + 1) * LANES] = y.astype(o.dtype)


def _rope(x, cos, sin, *, x_col=0, width, out_dtypes, tm=512):
    n = x.shape[0]
    assert n % tm == 0
    outs = pl.pallas_call(
        functools.partial(_rope_body, width=width),
        grid=(n // tm,),
        in_specs=[pl.BlockSpec((tm, 2 * width), lambda i: (i, x_col)),
                  pl.BlockSpec((tm, LANES), lambda i: (i, 0)),
                  pl.BlockSpec((tm, LANES), lambda i: (i, 0))],
        out_specs=[pl.BlockSpec((tm, width), lambda i: (i, 0)) for _ in out_dtypes],
        out_shape=[jax.ShapeDtypeStruct((n, width), dt) for dt in out_dtypes],
        compiler_params=_cparams(("parallel",)),
        name="rope",
    )(x, cos, sin)
    return outs[0] if len(out_dtypes) == 1 else outs


def _rope_tables(pos):
    half = MLA_ROPE // 2
    inv_freq = ROPE_BASE ** (-np.arange(half, dtype=np.float32) / half)
    ang = jnp.asarray(pos, F32)[:, None] * jnp.asarray(inv_freq)[None, :]
    zeros = jnp.zeros((ang.shape[0], LANES - MLA_ROPE), F32)
    cos = jnp.concatenate([jnp.cos(ang), jnp.cos(ang), zeros], axis=1)
    sin = jnp.concatenate([jnp.sin(ang), jnp.sin(ang), zeros], axis=1)
    return cos, sin


def _swap_halves(w):
    half = w.shape[-1] // 2
    return jnp.concatenate([-w[..., half:], w[..., :half]], axis=-1)


def _pad_lanes(w):
    return jnp.concatenate([w, jnp.zeros(w.shape[:-1] + (LANES - w.shape[-1],), w.dtype)], axis=-1)


def _mla_prompt_body(qn_ref, qr_ref, kn_ref, v_ref, kr_ref, o_ref, m_ref, l_ref, acc_ref, *, t):
    qi, ki = pl.program_id(1), pl.program_id(2)

    @pl.when(ki == 0)
    def _():
        _flash_init(m_ref, l_ref, acc_ref)

    def block(diagonal):
        if diagonal:
            allowed = ((lax.broadcasted_iota(jnp.int32, (t, t), 1) // CHUNK)
                       <= (lax.broadcasted_iota(jnp.int32, (t, t), 0) // CHUNK))
        kr = kr_ref[...]
        for h in range(MLA_HEADS):
            sl = slice(h * LANES, (h + 1) * LANES)
            q = jnp.concatenate([qn_ref[:, sl], qr_ref[:, sl]], axis=1)
            k = jnp.concatenate([kn_ref[:, sl], kr], axis=1)
            s = _qk(q, k)
            if diagonal:
                s = jnp.where(allowed, s, NEG_BIG)
            _flash_step(s, v_ref[:, sl], m_ref, l_ref, acc_ref, h, sl)

    @pl.when(ki < qi)
    def _():
        block(False)

    @pl.when(ki == qi)
    def _():
        block(True)
        _flash_finish(o_ref, l_ref, acc_ref, MLA_HEADS)


def _mla_attn_prompt(qn, qr, kv, kr, *, batch, seq, t=512):
    w = MLA_HEADS * LANES
    nt = seq // t
    assert seq % t == 0 and t % CHUNK == 0
    kv_row = lambda b, i, j: b * nt + jnp.minimum(i, j)
    return pl.pallas_call(
        functools.partial(_mla_prompt_body, t=t),
        grid=(batch, nt, nt),
        in_specs=[pl.BlockSpec((t, w), lambda b, i, j: (b * nt + i, 0)),
                  pl.BlockSpec((t, w), lambda b, i, j: (b * nt + i, 0)),
                  pl.BlockSpec((t, w), lambda b, i, j: (kv_row(b, i, j), 0)),
                  pl.BlockSpec((t, w), lambda b, i, j: (kv_row(b, i, j), 1)),
                  pl.BlockSpec((t, LANES), lambda b, i, j: (kv_row(b, i, j), 0))],
        out_specs=pl.BlockSpec((t, w), lambda b, i, j: (b * nt + i, 0)),
        out_shape=jax.ShapeDtypeStruct((batch * seq, w), BF16),
        scratch_shapes=_flash_scratch(t, MLA_HEADS),
        compiler_params=_cparams(("parallel", "parallel", "arbitrary")),
        name="mla_attn_prompt",
    )(qn, qr, kv, kv, kr)


def _mla_sample_body(qn_ref, qr_ref, cn_ref, krn_ref, cc_ref, krc_ref, wuk_ref, wuv_ref, o_ref,
                     *, past, tq):
    heads = MLA_HEADS
    q_lat = jnp.concatenate(
        [jnp.dot(qn_ref[:, h * LANES:(h + 1) * LANES], wuk_ref[h], preferred_element_type=F32)
         for h in range(heads)], axis=0).astype(BF16)
    q_r = jnp.concatenate([qr_ref[:, h * LANES:(h + 1) * LANES] for h in range(heads)], axis=0)
    c_c = cc_ref[0, 0].astype(BF16)
    kr_c = krc_ref[0, 0].astype(BF16)
    c_n = cn_ref[...]
    s_c = _qk(q_lat, c_c) + _qk(q_r[:, :MLA_ROPE], kr_c)
    s_n = _qk(q_lat, c_n) + _qk(q_r, krn_ref[...])
    vis_c = (np.arange(past)[None, :] // CHUNK) <= ((past + np.arange(tq))[:, None] // CHUNK)
    vis_n = ((past + np.arange(tq))[None, :] // CHUNK) <= ((past + np.arange(tq))[:, None] // CHUNK)
    if not vis_c.all():
        frame = lax.broadcasted_iota(jnp.int32, s_c.shape, 0) % tq
        key = lax.broadcasted_iota(jnp.int32, s_c.shape, 1)
        s_c = jnp.where((key // CHUNK) <= ((past + frame) // CHUNK), s_c, NEG_BIG)
    if not vis_n.all():
        frame = lax.broadcasted_iota(jnp.int32, s_n.shape, 0) % tq
        key = lax.broadcasted_iota(jnp.int32, s_n.shape, 1)
        s_n = jnp.where(((past + key) // CHUNK) <= ((past + frame) // CHUNK), s_n, NEG_BIG)
    m = jnp.maximum(jnp.max(s_c, axis=1, keepdims=True), jnp.max(s_n, axis=1, keepdims=True))
    p_c = jnp.exp2(s_c - m)
    p_n = jnp.exp2(s_n - m)
    l = jnp.sum(p_c, axis=1, keepdims=True) + jnp.sum(p_n, axis=1, keepdims=True)
    o_lat = (jnp.dot(p_c.astype(BF16), c_c, preferred_element_type=F32)
             + jnp.dot(p_n.astype(BF16), c_n, preferred_element_type=F32)) / l
    o_lat = o_lat.astype(BF16)
    for h in range(heads):
        o_ref[:, h * LANES:(h + 1) * LANES] = jnp.dot(
            o_lat[h * tq:(h + 1) * tq], wuv_ref[h], preferred_element_type=F32).astype(o_ref.dtype)


def _mla_attn_sample(qn, qr, ckv_new, kr_new, cache_ckv, cache_kr, layer, w_uk_t, w_uv, *, row0, batch, tq):
    w = MLA_HEADS * LANES
    past = cache_ckv.shape[2]
    rb0 = row0 // tq
    return pl.pallas_call(
        functools.partial(_mla_sample_body, past=past, tq=tq),
        grid=(batch,),
        in_specs=[pl.BlockSpec((tq, w), lambda b: (rb0 + b, 0)),
                  pl.BlockSpec((tq, w), lambda b: (rb0 + b, 0)),
                  pl.BlockSpec((tq, MLA_KV_LORA), lambda b: (rb0 + b, 0)),
                  pl.BlockSpec((tq, LANES), lambda b: (rb0 + b, 0)),
                  pl.BlockSpec((1, 1, past, MLA_KV_LORA), lambda b: (layer, b, 0, 0)),
                  pl.BlockSpec((1, 1, past, MLA_ROPE), lambda b: (layer, b, 0, 0)),
                  pl.BlockSpec((MLA_HEADS, MLA_NOPE, MLA_KV_LORA), lambda b: (0, 0, 0)),
                  pl.BlockSpec((MLA_HEADS, MLA_KV_LORA, MLA_V), lambda b: (0, 0, 0))],
        out_specs=pl.BlockSpec((tq, w), lambda b: (b, 0)),
        out_shape=jax.ShapeDtypeStruct((batch * tq, w), BF16),
        compiler_params=_cparams(("parallel",)),
        name="mla_attn_sample",
    )(qn, qr, ckv_new, kr_new, cache_ckv, cache_kr, w_uk_t, w_uv)


def _cross_body(q_ref, k_ref, v_ref, o_ref, *, groups, tq, n_mem, interleaved):
    for g in range(groups):
        rows = slice(g * tq, (g + 1) * tq)
        for h in range(MEM_HEADS):
            sl = slice(h * HEAD_DIM, (h + 1) * HEAD_DIM)
            if interleaved:
                k = k_ref[g, pl.ds(h, n_mem, stride=MEM_HEADS), :]
                v = v_ref[g, pl.ds(h, n_mem, stride=MEM_HEADS), :]
            else:
                k, v = k_ref[g, :, sl], v_ref[g, :, sl]
            s = _qk(q_ref[rows, sl], k.astype(BF16))
            p = jnp.exp2(s - jnp.max(s, axis=1, keepdims=True))
            l = jnp.sum(p, axis=1, keepdims=True)
            o = jnp.dot(p.astype(BF16), v.astype(BF16), preferred_element_type=F32)
            o_ref[rows, sl] = (o / l).astype(o_ref.dtype)


def _cross_attn(q, mem_k, mem_v, *, layer=None, row0, n_rows, rows_per_stream, groups, tq):
    wq = MEM_HEADS * HEAD_DIM
    step_rows = groups * tq
    stream_block = lambda i: i // (rows_per_stream // tq)
    rb0 = row0 // step_rows
    interleaved = mem_v is not None
    if interleaved:
        n_layers, n_streams, n_mem = mem_k.shape[:3]
        flat = (n_layers, n_streams, n_mem * MEM_HEADS, HEAD_DIM)
        spec = pl.BlockSpec((None, groups, n_mem * MEM_HEADS, HEAD_DIM), lambda i: (layer, stream_block(i), 0, 0))
        mem_specs, mem_args = [spec, spec], [mem_k.reshape(flat), mem_v.reshape(flat)]
    else:
        n_mem = mem_k.shape[1]
        mem_specs = [pl.BlockSpec((groups, n_mem, wq), lambda i: (stream_block(i), 0, 0)),
                     pl.BlockSpec((groups, n_mem, wq), lambda i: (stream_block(i), 0, 1))]
        mem_args = [mem_k, mem_k]
    return pl.pallas_call(
        functools.partial(_cross_body, groups=groups, tq=tq, n_mem=n_mem, interleaved=interleaved),
        grid=(n_rows // step_rows,),
        in_specs=[pl.BlockSpec((step_rows, wq), lambda i: (rb0 + i, 0))] + mem_specs,
        out_specs=pl.BlockSpec((step_rows, wq), lambda i: (i, 0)),
        out_shape=jax.ShapeDtypeStruct((n_rows, wq), BF16),
        compiler_params=_cparams(("parallel",)),
        name="cross_attn",
    )(q, *mem_args)


def _split_bf16(x):
    hi = x.astype(BF16)
    return hi, (x - hi.astype(F32)).astype(BF16)


def _router_body(x_ref, g_ref, w_ref, h_ref, idx_ref, gate_ref):
    h = _rms(x_ref[...], g_ref[...])
    h_ref[...] = h
    h_hi, h_lo = _split_bf16(h)
    w_hi, w_lo = _split_bf16(w_ref[...])
    logits = (jnp.dot(h_hi, w_hi, preferred_element_type=F32)
              + jnp.dot(h_hi, w_lo, preferred_element_type=F32)
              + jnp.dot(h_lo, w_hi, preferred_element_type=F32))
    lane = lax.broadcasted_iota(jnp.int32, logits.shape, 1)
    logits = jnp.where(lane < N_EXPERTS, logits, NEG_BIG)
    v1 = jnp.max(logits, axis=1, keepdims=True)
    i1 = jnp.min(jnp.where(logits == v1, lane, LANES), axis=1, keepdims=True)
    rest = jnp.where(lane == i1, NEG_BIG, logits)
    v2 = jnp.max(rest, axis=1, keepdims=True)
    i2 = jnp.min(jnp.where(rest == v2, lane, LANES), axis=1, keepdims=True)
    e2 = jnp.exp(v2 - v1)
    g1 = 1.0 / (1.0 + e2)
    g2 = e2 / (1.0 + e2)
    idx_ref[...] = jnp.where(lane == 0, i1, jnp.where(lane == 1, i2, 0))
    gate_ref[...] = jnp.where(lane == 0, g1, jnp.where(lane == 1, g2, 0.0))


def _router(x, gain, w_router, *, tm=512):
    n, d = x.shape
    assert n % tm == 0
    return pl.pallas_call(
        _router_body,
        grid=(n // tm,),
        in_specs=[pl.BlockSpec((tm, d), lambda i: (i, 0)),
                  pl.BlockSpec((1, d), lambda i: (0, 0)),
                  pl.BlockSpec((d, LANES), lambda i: (0, 0))],
        out_specs=[pl.BlockSpec((tm, d), lambda i: (i, 0)),
                   pl.BlockSpec((tm, LANES), lambda i: (i, 0)),
                   pl.BlockSpec((tm, LANES), lambda i: (i, 0))],
        out_shape=[jax.ShapeDtypeStruct((n, d), F32),
                   jax.ShapeDtypeStruct((n, LANES), jnp.int32),
                   jax.ShapeDtypeStruct((n, LANES), F32)],
        compiler_params=_cparams(("parallel",)),
        name="router",
    )(x, gain.reshape(1, d).astype(F32), _pad_lanes(w_router))


ROW_DMA_UNROLL = 8


def _gather_body(idx_ref, src_ref, out_ref, buf_ref, sem, *, rows, n_steps):
    i = pl.program_id(0)

    def issue_block(blk):
        slot = blk % 2
        base = blk * rows

        def issue(g, carry):
            for u in range(ROW_DMA_UNROLL):
                r = g * ROW_DMA_UNROLL + u
                pltpu.make_async_copy(src_ref.at[pl.ds(idx_ref[base + r], 1)],
                                      buf_ref.at[slot, pl.ds(r, 1)], sem.at[slot]).start(priority=u % 2)
            return carry

        lax.fori_loop(0, rows // ROW_DMA_UNROLL, issue, 0)

    @pl.when(i == 0)
    def _():
        issue_block(i)

    @pl.when(i + 1 < n_steps)
    def _():
        issue_block(i + 1)

    slot = i % 2
    pltpu.make_async_copy(src_ref.at[pl.ds(0, rows)], buf_ref.at[slot], sem.at[slot]).wait()
    out_ref[...] = buf_ref[slot].astype(out_ref.dtype)


def _gather_rows(src, row_idx, *, out_dtype, rows=512):
    r, d = row_idx.shape[0], src.shape[1]
    assert r % rows == 0 and rows % ROW_DMA_UNROLL == 0 and src.dtype == F32
    return pl.pallas_call(
        functools.partial(_gather_body, rows=rows, n_steps=r // rows),
        grid_spec=pltpu.PrefetchScalarGridSpec(
            num_scalar_prefetch=1, grid=(r // rows,),
            in_specs=[pl.BlockSpec(memory_space=pl.ANY)],
            out_specs=pl.BlockSpec((rows, d), lambda i, idx: (i, 0)),
            scratch_shapes=[pltpu.VMEM((2, rows, d), F32), pltpu.SemaphoreType.DMA((2,))]),
        out_shape=jax.ShapeDtypeStruct((r, d), out_dtype),
        compiler_params=_cparams(("arbitrary",)),
        name="gather_rows",
    )(row_idx, src)


def _moe_body(wt_ref, we_ref, ws_ref, wc_ref, wf_ref, x_ref, wg_ref, wu_ref, wd_ref, out_ref, *, ts, nsub):
    w, f = pl.program_id(0), pl.program_id(1)

    @pl.when((f == 0) & (wf_ref[w] == 1))
    def _():
        out_ref[...] = jnp.zeros(out_ref.shape, F32)

    s0, cnt = ws_ref[w], wc_ref[w]
    for a in range(nsub):
        for c in range(1, nsub - a + 1):
            @pl.when((s0 == a) & (cnt == c))
            def _():
                rows = slice(a * ts, (a + c) * ts)
                out_ref[rows, :] += _swiglu_partial(x_ref[rows, :], wg_ref[...], wu_ref[...], wd_ref[...])


def _moe_ffn(xs, w_gu, w_d, layer, items, *, tm, ts, tf=512):
    r, d = xs.shape
    f = w_d.shape[2]
    nf = f // tf
    n_items = items[0].shape[0]
    assert r % tm == 0 and tm % ts == 0 and f % tf == 0

    def fidx(w, j, wc):
        return jnp.where(wc[w] > 0, j, nf - 1)

    return pl.pallas_call(
        functools.partial(_moe_body, ts=ts, nsub=tm // ts),
        grid_spec=pltpu.PrefetchScalarGridSpec(
            num_scalar_prefetch=5, grid=(n_items, nf),
            in_specs=[pl.BlockSpec((tm, d), lambda w, j, wt, we, ws, wc, wf: (wt[w], 0)),
                      pl.BlockSpec((None, None, d, tf),
                                   lambda w, j, wt, we, ws, wc, wf: (layer, we[w], 0, fidx(w, j, wc))),
                      pl.BlockSpec((None, None, d, tf),
                                   lambda w, j, wt, we, ws, wc, wf: (layer, we[w], 0, nf + fidx(w, j, wc))),
                      pl.BlockSpec((None, None, tf, d),
                                   lambda w, j, wt, we, ws, wc, wf: (layer, we[w], fidx(w, j, wc), 0))],
            out_specs=pl.BlockSpec((tm, d), lambda w, j, wt, we, ws, wc, wf: (wt[w], 0))),
        out_shape=jax.ShapeDtypeStruct((r, d), F32),
        compiler_params=_cparams(("arbitrary", "arbitrary")),
        name="moe_ffn",
    )(*items, xs, w_gu, w_gu, w_d)


def _combine_body(p0_ref, p1_ref, x_ref, gate_ref, ys_ref, out_ref, buf_ref, sem, *, rows, n_steps):
    i = pl.program_id(0)

    def issue_block(blk):
        slot = blk % 2
        base = blk * rows

        def issue(g, carry):
            for u in range(ROW_DMA_UNROLL):
                r = g * ROW_DMA_UNROLL + u
                pltpu.make_async_copy(ys_ref.at[pl.ds(p0_ref[base + r], 1)],
                                      buf_ref.at[slot, 0, pl.ds(r, 1)], sem.at[slot]).start(priority=0)
                pltpu.make_async_copy(ys_ref.at[pl.ds(p1_ref[base + r], 1)],
                                      buf_ref.at[slot, 1, pl.ds(r, 1)], sem.at[slot]).start(priority=1)
            return carry

        lax.fori_loop(0, rows // ROW_DMA_UNROLL, issue, 0)

    @pl.when(i == 0)
    def _():
        issue_block(i)

    @pl.when(i + 1 < n_steps)
    def _():
        issue_block(i + 1)

    slot = i % 2
    pltpu.make_async_copy(ys_ref.at[pl.ds(0, rows)], buf_ref.at[slot, 0], sem.at[slot]).wait()
    pltpu.make_async_copy(ys_ref.at[pl.ds(0, rows)], buf_ref.at[slot, 1], sem.at[slot]).wait()
    g = gate_ref[...]
    out_ref[...] = x_ref[...] + (g[:, 0:1] * buf_ref[slot, 0] + g[:, 1:2] * buf_ref[slot, 1])


def _moe_combine(x, gates, ys, pos0, pos1, *, rows=256):
    n, d = x.shape
    assert n % rows == 0 and rows % ROW_DMA_UNROLL == 0
    return pl.pallas_call(
        functools.partial(_combine_body, rows=rows, n_steps=n // rows),
        grid_spec=pltpu.PrefetchScalarGridSpec(
            num_scalar_prefetch=2, grid=(n // rows,),
            in_specs=[pl.BlockSpec((rows, d), lambda i, p0, p1: (i, 0)),
                      pl.BlockSpec((rows, LANES), lambda i, p0, p1: (i, 0)),
                      pl.BlockSpec(memory_space=pl.ANY)],
            out_specs=pl.BlockSpec((rows, d), lambda i, p0, p1: (i, 0)),
            scratch_shapes=[pltpu.VMEM((2, 2, rows, d), F32), pltpu.SemaphoreType.DMA((2,))]),
        out_shape=jax.ShapeDtypeStruct((n, d), F32),
        compiler_params=_cparams(("arbitrary",)),
        name="moe_combine",
    )(pos0, pos1, x, gates, ys)


def _moe_plan(idx, *, tm, ts):
    n = idx.shape[0]
    nsub = tm // ts
    cap = -(-(2 * n + N_EXPERTS * ts) // tm) * tm
    n_tiles, n_subs = cap // tm, cap // ts
    n_items = n_tiles + N_EXPERTS - 1
    e_flat = idx.reshape(-1)
    onehot = (e_flat[:, None] == jnp.arange(N_EXPERTS, dtype=jnp.int32)[None, :]).astype(jnp.int32)
    before = jnp.cumsum(onehot, axis=0) - onehot
    rank = jnp.sum(before * onehot, axis=1)
    counts = jnp.sum(onehot, axis=0)
    padded = ((counts + ts - 1) // ts) * ts
    g_end = jnp.cumsum(padded)
    g_start = g_end - padded
    pos = g_start[e_flat] + rank
    row_token = jnp.zeros((cap,), jnp.int32).at[pos].set(jnp.arange(2 * n, dtype=jnp.int32) // 2)
    s = jnp.arange(n_subs, dtype=jnp.int32)
    sub_e = jnp.sum((s[:, None] * ts >= g_end[None, :]).astype(jnp.int32), axis=1)
    valid = sub_e < N_EXPERTS
    prev_e = jnp.concatenate([jnp.full((1,), -1, jnp.int32), sub_e[:-1]])
    tile_start = (s % nsub) == 0
    new = tile_start | (valid & (sub_e != prev_e))
    item_of_sub = jnp.cumsum(new.astype(jnp.int32)) - 1
    n_used = item_of_sub[-1] + 1
    tgt = jnp.where(new, item_of_sub, n_items)
    w = jnp.arange(n_items, dtype=jnp.int32)
    used = w < n_used
    it_tile = jnp.full((n_items,), n_tiles - 1, jnp.int32).at[tgt].set(s // nsub, mode="drop")
    it_exp = jnp.full((n_items,), N_EXPERTS - 1, jnp.int32).at[tgt].set(
        jnp.minimum(sub_e, N_EXPERTS - 1), mode="drop")
    it_s0 = jnp.zeros((n_items,), jnp.int32).at[tgt].set(s % nsub, mode="drop")
    it_first = jnp.zeros((n_items,), jnp.int32).at[tgt].set(tile_start.astype(jnp.int32), mode="drop")
    it_cnt = jnp.zeros((n_items,), jnp.int32).at[item_of_sub].add(valid.astype(jnp.int32), mode="drop")
    it_cnt = jnp.where(used, it_cnt, 0)
    return pos[0::2], pos[1::2], row_token, (it_tile, it_exp, it_s0, it_cnt, it_first)


def _moe_layer(x, gain, w_router, w_gu, w_d, layer, *, tm=1024, ts=256):
    h, idx, gates = _router(x, gain, w_router)
    pos0, pos1, row_token, items = _moe_plan(idx[:, :2], tm=tm, ts=ts)
    xs = _gather_rows(h, row_token, out_dtype=BF16)
    ys = _moe_ffn(xs, w_gu, w_d, layer, items, tm=tm, ts=ts)
    return _moe_combine(x, gates, ys, pos0, pos1)


def _fox_layer(x, gain, w_in, b_f, w_o, cache_k, cache_v, cache_logf, layer, *, bp, tp, bs, tq):
    n_p = bp * tp
    w = FOX_HEADS * HEAD_DIM
    q_scale = jnp.full((w,), HEAD_DIM ** -0.5 * LOG2E, F32)
    q = _matmul(x, w_in, layer=layer, gain=gain, col_scale=q_scale, m=w, out_dtypes=(BF16,))
    kv, kv_b = _matmul(x, w_in, layer=layer, gain=gain, w_col0=w, m=2 * w, out_dtypes=(F32, BF16))
    logf = _matmul(x, _pad_lanes(w_in[layer, :, 3 * w:]), gain=gain, bias=_pad_lanes(b_f),
                   act="log_sigmoid")[:, :FOX_HEADS]
    logf_p = logf[:n_p].reshape(bp, tp, FOX_HEADS)
    logf_s = logf[n_p:].reshape(bs, tq, FOX_HEADS)
    past = cache_logf.shape[2]
    cs_p = _cumsum_time(logf_p)
    cs_s = _cumsum_time(jnp.concatenate([cache_logf[layer], logf_s], axis=1))
    aq, ak = _fox_aug(cs_p.reshape(n_p, FOX_HEADS))
    o_p = _fox_attn_prompt(q, kv_b, aq, ak, batch=bp, seq=tp)
    o_s = _fox_attn_sample(q, kv_b, cache_k, cache_v, layer, cs_s[:, past:], cs_s.transpose(0, 2, 1),
                           row0=n_p, batch=bs, tq=tq)
    x = _matmul((o_p, o_s), w_o, layer=layer, res=x)
    return x, kv[:, :w], kv[:, w:], logf


def _mla_layer(x, gain, w_in, g_q, g_kv, w_uq, w_ukv, w_o, layer, cache_ckv, cache_kr, cos, sin,
               *, bp, tp, bs, tq):
    n_p = bp * tp
    h, c, r = MLA_HEADS, MLA_KV_LORA, MLA_ROPE
    w_kr = w_in[:, MLA_Q_LORA + c:]
    w_in_big = jnp.concatenate([w_in[:, :MLA_Q_LORA + c], _pad_lanes(w_kr), _pad_lanes(_swap_halves(w_kr))], axis=1)
    w_q = w_uq.reshape(MLA_Q_LORA, h, MLA_NOPE + r)
    w_qr = w_q[:, :, MLA_NOPE:]
    w_uq_big = jnp.concatenate([w_q[:, :, :MLA_NOPE].reshape(MLA_Q_LORA, h * MLA_NOPE),
                                _pad_lanes(w_qr).reshape(MLA_Q_LORA, h * LANES),
                                _pad_lanes(_swap_halves(w_qr)).reshape(MLA_Q_LORA, h * LANES)], axis=1)
    w_kv = w_ukv.reshape(c, h, MLA_NOPE + MLA_V)
    w_ukv_split = jnp.concatenate([w_kv[:, :, :MLA_NOPE].reshape(c, h * MLA_NOPE),
                                   w_kv[:, :, MLA_NOPE:].reshape(c, h * MLA_V)], axis=1)
    w_uk_t = w_kv[:, :, :MLA_NOPE].transpose(1, 2, 0).astype(BF16)
    w_uv = w_kv[:, :, MLA_NOPE:].transpose(1, 0, 2).astype(BF16)
    q_scale = (MLA_NOPE + r) ** -0.5 * LOG2E

    a = _matmul(x, w_in_big, gain=gain, tn=256)
    ckv, ckv_b = _rmsnorm(a, g_kv, x_col=1, k=c, out_dtypes=(F32, BF16))
    kr, kr_b = _rope(a, cos, sin, x_col=(MLA_Q_LORA + c) // (2 * LANES), width=LANES, out_dtypes=(F32, BF16))
    qn = _matmul(a, w_uq_big, gain=g_q, x_col=0, k=MLA_Q_LORA, m=h * MLA_NOPE,
                 col_scale=jnp.full((h * MLA_NOPE,), q_scale, F32), out_dtypes=(BF16,))
    qr2 = _matmul(a, w_uq_big, gain=g_q, x_col=0, k=MLA_Q_LORA, w_col0=h * MLA_NOPE,
                  col_scale=jnp.full((2 * h * LANES,), q_scale, F32))
    qr = _rope(qr2, cos, sin, width=h * LANES, out_dtypes=(BF16,))
    kv = _matmul(ckv_b, w_ukv_split, n_rows=n_p, out_dtypes=(BF16,))
    o_p = _mla_attn_prompt(qn, qr, kv, kr_b, batch=bp, seq=tp)
    o_s = _mla_attn_sample(qn, qr, ckv_b, kr_b, cache_ckv, cache_kr, layer, w_uk_t, w_uv,
                           row0=n_p, batch=bs, tq=tq)
    x = _matmul((o_p, o_s), w_o, layer=layer, res=x)
    return x, ckv, kr[:, :r]


def _cross_layer(x, gain, w_q, w_o, layer, kv_p, cache_k, cache_v, *, bp, tp, bs, tq):
    n_p = bp * tp
    wq = MEM_HEADS * HEAD_DIM
    q = _matmul(x, w_q, layer=layer, gain=gain, col_scale=jnp.full((wq,), HEAD_DIM ** -0.5 * LOG2E, F32),
                out_dtypes=(BF16,))
    o_p = _cross_attn(q, kv_p, None, row0=0, n_rows=n_p, rows_per_stream=tp, groups=1, tq=1024)
    o_s = _cross_attn(q, cache_k, cache_v, layer=layer, row0=n_p, n_rows=bs * tq, rows_per_stream=tq,
                      groups=8, tq=tq)
    return _matmul((o_p, o_s), w_o, layer=layer, res=x)


def kernel(x_prompt, x_sample, cache_fox_k, cache_fox_v, cache_fox_logf, cache_mla_ckv, cache_mla_krope, cache_mem_k, cache_mem_v, mem_prompt, norm_mix, norm_cross, norm_ffn, norm_final, fox_w_in, fox_b_f, fox_w_o, mla_w_in, mla_g_q, mla_g_kv, mla_w_uq, mla_w_ukv, mla_w_o, cross_w_q, cross_w_kv, cross_w_o, ffn_w_gu, ffn_w_d, moe_w_router, moe_w_gu, moe_w_d):
    bp, tp, d = x_prompt.shape
    bs, tq, _ = x_sample.shape
    n_p = bp * tp
    depth = norm_mix.shape[0]
    past = cache_fox_k.shape[2]
    n_mem = mem_prompt.shape[1]
    mem_w = MEM_HEADS * HEAD_DIM
    dims = dict(bp=bp, tp=tp, bs=bs, tq=tq)

    x = jnp.concatenate([x_prompt.reshape(n_p, d), x_sample.reshape(bs * tq, d)], axis=0)
    pos = np.concatenate([np.tile(np.arange(tp), bp), np.tile(past + np.arange(tq), bs)])
    cos, sin = _rope_tables(pos)
    mem_flat = mem_prompt.reshape(bp * n_mem, d)

    fox_rows, mla_rows, mem_rows = [], [], []
    for i in range(depth):
        j = i // 2
        if i % 2 == 0:
            x, k, v, lf = _fox_layer(x, norm_mix[i], fox_w_in, fox_b_f[j], fox_w_o,
                                     cache_fox_k, cache_fox_v, cache_fox_logf, j, **dims)
            fox_rows.append((k, v, lf))
        else:
            x, ckv, kr = _mla_layer(x, norm_mix[i], mla_w_in[j], mla_g_q[j], mla_g_kv[j], mla_w_uq[j],
                                    mla_w_ukv[j], mla_w_o, j, cache_mla_ckv, cache_mla_krope,
                                    cos, sin, **dims)
            mla_rows.append((ckv, kr))
        kv_p = _matmul(mem_flat, cross_w_kv, layer=i).reshape(bp, n_mem, 2 * mem_w)
        mem_rows.append((kv_p[:, :, :mem_w], kv_p[:, :, mem_w:]))
        x = _cross_layer(x, norm_cross[i], cross_w_q, cross_w_o, i, kv_p, cache_mem_k, cache_mem_v, **dims)
        if i % 2 == 0:
            x = _ffn(x, norm_ffn[i], ffn_w_gu, ffn_w_d, j)
        else:
            x = _moe_layer(x, norm_ffn[i], moe_w_router[j], moe_w_gu, moe_w_d, j)
    y_p = _rmsnorm(x, norm_final, n_rows=n_p).reshape(bp, tp, d)
    y_s = _rmsnorm(x, norm_final, row0=n_p).reshape(bs, tq, d)

    def split(rows, tail):
        return rows[:n_p].reshape((bp, tp) + tail), rows[n_p:].reshape((bs, tq) + tail)

    def stacked(rows_list, tail):
        parts = [split(r, tail) for r in rows_list]
        return jnp.stack([p[0] for p in parts]), jnp.stack([p[1] for p in parts])

    hd = (FOX_HEADS, HEAD_DIM)
    fox_k_p, fox_k_s = stacked([r[0] for r in fox_rows], hd)
    fox_v_p, fox_v_s = stacked([r[1] for r in fox_rows], hd)
    fox_lf_p, fox_lf_s = stacked([r[2] for r in fox_rows], (FOX_HEADS,))
    mla_ckv_p, mla_ckv_s = stacked([r[0] for r in mla_rows], (MLA_KV_LORA,))
    mla_kr_p, mla_kr_s = stacked([r[1] for r in mla_rows], (MLA_ROPE,))
    mem_k_p = jnp.stack([m[0].reshape(bp, n_mem, MEM_HEADS, HEAD_DIM) for m in mem_rows])
    mem_v_p = jnp.stack([m[1].reshape(bp, n_mem, MEM_HEADS, HEAD_DIM) for m in mem_rows])
    return (y_p, y_s, fox_k_p, fox_v_p, fox_lf_p, mla_ckv_p, mla_kr_p, mem_k_p, mem_v_p,
            fox_k_s, fox_v_s, fox_lf_s, mla_ckv_s, mla_kr_s)
```

```python
import functools

import numpy as np
import jax
import jax.numpy as jnp
from jax import lax
from jax.experimental import pallas as pl
from jax.experimental.pallas import tpu as pltpu

F32 = jnp.float32
BF16 = jnp.bfloat16

D_MODEL = 2048
CHUNK = 64
RMS_EPS = 1e-6
FOX_HEADS = 16
HEAD_DIM = 128
MLA_HEADS = 16
MLA_Q_LORA = 512
MLA_KV_LORA = 512
MLA_NOPE = 128
MLA_ROPE = 64
MLA_V = 128
ROPE_BASE = 10000.0
MEM_HEADS = 4
N_EXPERTS = 8
LANES = 128

VMEM_LIMIT_BYTES = 56 * 1024 * 1024
NEG_BIG = -1e30
LOG2E = 1.4426950408889634


def _cparams(sem):
    return pltpu.CompilerParams(dimension_semantics=sem, vmem_limit_bytes=VMEM_LIMIT_BYTES)


def _rms(xf, g):
    return xf * lax.rsqrt(jnp.mean(xf * xf, axis=-1, keepdims=True) + RMS_EPS) * g


def _log_sigmoid(x):
    return jnp.minimum(x, 0.0) - jnp.log1p(jnp.exp(-jnp.abs(x)))


def _split3_bf16(x):
    hi = x.astype(BF16)
    r1 = x - hi.astype(F32)
    mid = r1.astype(BF16)
    lo = (r1 - mid.astype(F32)).astype(BF16)
    return hi, mid, lo


def _mm_body(*refs, has_norm, has_bias, has_scale, has_res, act, use_scratch, n_out, split):
    it = iter(refs)
    x_ref = next(it)
    x2_ref = next(it) if split is not None else None
    g_ref = next(it) if has_norm else None
    w_ref = next(it)
    b_ref = next(it) if has_bias else None
    s_ref = next(it) if has_scale else None
    r_ref = next(it) if has_res else None
    out_refs = [next(it) for _ in range(n_out)]
    xb_ref = next(it) if use_scratch else None

    def stage(src_ref):
        xf = src_ref[...].astype(F32)
        if has_norm:
            xf = _rms(xf, g_ref[...])
        xb_ref[...] = xf.astype(BF16)

    if use_scratch:
        first = pl.program_id(1) == 0
        if split is None:
            pl.when(first)(lambda: stage(x_ref))
        else:
            pl.when(first & (pl.program_id(0) < split))(lambda: stage(x_ref))
            pl.when(first & (pl.program_id(0) >= split))(lambda: stage(x2_ref))
        xb = xb_ref[...]
    else:
        xb = x_ref[...]
    acc = jnp.dot(xb, w_ref[...].astype(BF16), preferred_element_type=F32)
    if has_bias:
        acc = acc + b_ref[...]
    if act == "log_sigmoid":
        acc = _log_sigmoid(acc)
    if has_scale:
        acc = acc * s_ref[...]
    if has_res:
        acc = acc + r_ref[...]
    for o in out_refs:
        o[...] = acc.astype(o.dtype)


def _matmul(x, w, *, layer=None, gain=None, bias=None, col_scale=None, res=None, act=None, x_col=0,
            k=None, w_col0=0, m=None, n_rows=None, out_dtypes=(F32,), tm=1024, tn=1024):
    x, x2 = x if isinstance(x, tuple) else (x, None)
    n = (x.shape[0] + (0 if x2 is None else x2.shape[0])) if n_rows is None else n_rows
    k = x.shape[1] if k is None else k
    m = w.shape[-1] - w_col0 if m is None else m
    tm = min(tm, n)
    tn = next(c for c in (tn, 512, 256, 128, m) if m % c == 0 and w_col0 % c == 0)
    assert n % tm == 0 and w.shape[-2] == k
    assert (w.ndim == 3) == (layer is not None)
    has_norm, has_bias, has_res = gain is not None, bias is not None, res is not None
    has_scale = col_scale is not None
    use_scratch = has_norm or x.dtype != BF16 or x2 is not None
    col0 = w_col0 // tn
    row_vec = pl.BlockSpec((1, tn), lambda i, j: (0, j))
    if x2 is None:
        split = None
        in_specs = [pl.BlockSpec((tm, k), lambda i, j: (i, x_col))]
        args = [x]
    else:
        assert x.shape[0] % tm == 0 and x2.shape[0] % tm == 0 and n_rows is None
        split = x.shape[0] // tm
        in_specs = [pl.BlockSpec((tm, k), lambda i, j: (jnp.minimum(i, split - 1), x_col)),
                    pl.BlockSpec((tm, k), lambda i, j: (jnp.maximum(i - split, 0), x_col))]
        args = [x, x2]
    if has_norm:
        in_specs.append(pl.BlockSpec((1, k), lambda i, j: (0, 0)))
        args.append(gain.reshape(1, k).astype(F32))
    if layer is None:
        in_specs.append(pl.BlockSpec((k, tn), lambda i, j: (0, j + col0)))
    else:
        in_specs.append(pl.BlockSpec((None, k, tn), lambda i, j: (layer, 0, j + col0)))
    args.append(w)
    if has_bias:
        in_specs.append(row_vec)
        args.append(bias.reshape(1, m).astype(F32))
    if has_scale:
        in_specs.append(row_vec)
        args.append(col_scale.reshape(1, m).astype(F32))
    if has_res:
        in_specs.append(pl.BlockSpec((tm, tn), lambda i, j: (i, j)))
        args.append(res)
    body = functools.partial(_mm_body, has_norm=has_norm, has_bias=has_bias, has_scale=has_scale,
                             has_res=has_res, act=act, use_scratch=use_scratch, n_out=len(out_dtypes),
                             split=split)
    outs = pl.pallas_call(
        body,
        grid=(n // tm, m // tn),
        in_specs=in_specs,
        out_specs=[pl.BlockSpec((tm, tn), lambda i, j: (i, j)) for _ in out_dtypes],
        out_shape=[jax.ShapeDtypeStruct((n, m), dt) for dt in out_dtypes],
        scratch_shapes=[pltpu.VMEM((tm, k), BF16)] if use_scratch else [],
        compiler_params=_cparams(("parallel", "arbitrary")),
        name="matmul",
    )(*args)
    return outs[0] if len(out_dtypes) == 1 else outs


def _norm_body(x_ref, g_ref, *out_refs):
    y = _rms(x_ref[...].astype(F32), g_ref[...])
    for o in out_refs:
        o[...] = y.astype(o.dtype)


def _rmsnorm(x, gain, *, x_col=0, k=None, row0=0, n_rows=None, out_dtypes=(F32,), tm=1024):
    n = x.shape[0] - row0 if n_rows is None else n_rows
    k = x.shape[1] if k is None else k
    tm = min(tm, n)
    assert n % tm == 0 and row0 % tm == 0
    rb0 = row0 // tm
    outs = pl.pallas_call(
        _norm_body,
        grid=(n // tm,),
        in_specs=[pl.BlockSpec((tm, k), lambda i: (rb0 + i, x_col)),
                  pl.BlockSpec((1, k), lambda i: (0, 0))],
        out_specs=[pl.BlockSpec((tm, k), lambda i: (i, 0)) for _ in out_dtypes],
        out_shape=[jax.ShapeDtypeStruct((n, k), dt) for dt in out_dtypes],
        compiler_params=_cparams(("parallel",)),
        name="rmsnorm",
    )(x, gain.reshape(1, k).astype(F32))
    return outs[0] if len(out_dtypes) == 1 else outs


def _swiglu_partial(xb, wg, wu, wd):
    g = jnp.dot(xb, wg.astype(BF16), preferred_element_type=F32)
    u = jnp.dot(xb, wu.astype(BF16), preferred_element_type=F32)
    a = (g * jax.nn.sigmoid(g)) * u
    return jnp.dot(a.astype(BF16), wd.astype(BF16), preferred_element_type=F32)


def _ffn_body(x_ref, g_ref, wg_ref, wu_ref, wd_ref, out_ref, xb_ref):
    @pl.when(pl.program_id(1) == 0)
    def _():
        xf = x_ref[...]
        xb_ref[...] = _rms(xf, g_ref[...]).astype(BF16)
        out_ref[...] = xf
    out_ref[...] += _swiglu_partial(xb_ref[...], wg_ref[...], wu_ref[...], wd_ref[...])


def _ffn(x, gain, w_gu, w_d, layer, *, tm=1024, tf=256):
    n, d = x.shape
    f = w_d.shape[1]
    assert n % tm == 0 and f % tf == 0
    nf = f // tf
    return pl.pallas_call(
        _ffn_body,
        grid=(n // tm, nf),
        in_specs=[pl.BlockSpec((tm, d), lambda i, j: (i, 0)),
                  pl.BlockSpec((1, d), lambda i, j: (0, 0)),
                  pl.BlockSpec((None, d, tf), lambda i, j: (layer, 0, j)),
                  pl.BlockSpec((None, d, tf), lambda i, j: (layer, 0, j + nf)),
                  pl.BlockSpec((None, tf, d), lambda i, j: (layer, j, 0))],
        out_specs=pl.BlockSpec((tm, d), lambda i, j: (i, 0)),
        out_shape=jax.ShapeDtypeStruct((n, d), F32),
        scratch_shapes=[pltpu.VMEM((tm, d), BF16)],
        compiler_params=_cparams(("parallel", "arbitrary")),
        name="ffn",
    )(x, gain.reshape(1, d).astype(F32), w_gu, w_gu, w_d)


def _cumsum_body(x_ref, out_ref, *, chunk):
    t = x_ref.shape[1]
    row = lax.broadcasted_iota(jnp.int32, (chunk, chunk), 0)
    col = lax.broadcasted_iota(jnp.int32, (chunk, chunk), 1)
    tri = jnp.where(col <= row, 1.0, 0.0).astype(BF16)
    carry = jnp.zeros((1, x_ref.shape[2]), F32)
    for c in range(t // chunk):
        hi, mid, lo = _split3_bf16(x_ref[0, c * chunk:(c + 1) * chunk, :])
        cs = (jnp.dot(tri, hi, preferred_element_type=F32)
              + jnp.dot(tri, mid, preferred_element_type=F32)
              + jnp.dot(tri, lo, preferred_element_type=F32)) + carry
        out_ref[0, c * chunk:(c + 1) * chunk, :] = cs
        carry = cs[chunk - 1:chunk, :]


def _cumsum_time(x):
    b, t, h = x.shape
    chunk = next(c for c in (256, 128, 64, 32, 16, 8) if t % c == 0)
    return pl.pallas_call(
        functools.partial(_cumsum_body, chunk=chunk),
        grid=(b,),
        in_specs=[pl.BlockSpec((1, t, h), lambda i: (i, 0, 0))],
        out_specs=pl.BlockSpec((1, t, h), lambda i: (i, 0, 0)),
        out_shape=jax.ShapeDtypeStruct((b, t, h), F32),
        compiler_params=_cparams(("parallel",)),
        name="cumsum",
    )(x)


def _qk(q, k):
    return lax.dot_general(q, k, (((1,), (1,)), ((), ())), preferred_element_type=F32)


def _flash_step(s, v, m_ref, l_ref, acc_ref, h, lanes, rows=slice(None)):
    m_old = m_ref[h, rows]
    m_new = jnp.maximum(m_old, jnp.max(s, axis=1, keepdims=True))
    alpha = jnp.exp2(m_old - m_new)
    p = jnp.exp2(s - jnp.concatenate([m_new] * (s.shape[1] // LANES), axis=1))
    l_ref[h, rows] = alpha * l_ref[h, rows] + jnp.sum(p, axis=1, keepdims=True)
    acc_ref[rows, lanes] = alpha * acc_ref[rows, lanes] + jnp.dot(p.astype(BF16), v, preferred_element_type=F32)
    m_ref[h, rows] = m_new


def _flash_diagonal(q_of, k_of, v_of, allowed_half, t, m_ref, l_ref, acc_ref, h, lanes):
    lo, hi = slice(0, t // 2), slice(t // 2, t)
    s = jnp.where(allowed_half, _qk(q_of(lo), k_of(lo)), NEG_BIG)
    _flash_step(s, v_of(lo), m_ref, l_ref, acc_ref, h, lanes, lo)
    _flash_step(_qk(q_of(hi), k_of(lo)), v_of(lo), m_ref, l_ref, acc_ref, h, lanes, hi)
    s = jnp.where(allowed_half, _qk(q_of(hi), k_of(hi)), NEG_BIG)
    _flash_step(s, v_of(hi), m_ref, l_ref, acc_ref, h, lanes, hi)


def _flash_init(m_ref, l_ref, acc_ref):
    m_ref[...] = jnp.full(m_ref.shape, NEG_BIG, F32)
    l_ref[...] = jnp.zeros(l_ref.shape, F32)
    acc_ref[...] = jnp.zeros(acc_ref.shape, F32)


def _flash_finish(o_ref, l_ref, acc_ref, heads):
    for h in range(heads):
        sl = slice(h * LANES, (h + 1) * LANES)
        o_ref[:, sl] = (acc_ref[:, sl] / l_ref[h]).astype(o_ref.dtype)


def _flash_scratch(t, heads):
    return [pltpu.VMEM((heads, t, LANES), F32), pltpu.VMEM((heads, t, LANES), F32),
            pltpu.VMEM((t, heads * LANES), F32)]


def _fox_aug_body(cs_ref, aq_ref, ak_ref):
    c = cs_ref[...] * LOG2E
    shape = (c.shape[0], LANES)
    lane = lax.broadcasted_iota(jnp.int32, shape, 1)
    for h in range(FOX_HEADS):
        hi, mid, lo = (p.astype(F32) for p in _split3_bf16(jnp.broadcast_to(c[:, h:h + 1], shape)))
        aq = jnp.where(lane == 0, hi, jnp.where(lane == 1, mid, jnp.where(lane == 2, lo,
                       jnp.where(lane < 6, 1.0, 0.0))))
        ak = jnp.where(lane < 3, 1.0, jnp.where(lane == 3, -hi, jnp.where(lane == 4, -mid,
                       jnp.where(lane == 5, -lo, 0.0))))
        sl = slice(h * LANES, (h + 1) * LANES)
        aq_ref[:, sl] = aq.astype(BF16)
        ak_ref[:, sl] = ak.astype(BF16)


def _fox_aug(csum_rows, *, tm=512):
    n, h = csum_rows.shape
    assert n % tm == 0
    return pl.pallas_call(
        _fox_aug_body,
        grid=(n // tm,),
        in_specs=[pl.BlockSpec((tm, h), lambda i: (i, 0))],
        out_specs=[pl.BlockSpec((tm, h * LANES), lambda i: (i, 0))] * 2,
        out_shape=[jax.ShapeDtypeStruct((n, h * LANES), BF16)] * 2,
        compiler_params=_cparams(("parallel",)),
        name="fox_aug",
    )(csum_rows)


def _fox_prompt_body(q_ref, aq_ref, k_ref, v_ref, ak_ref, o_ref, m_ref, l_ref, acc_ref, *, t):
    qi, ki = pl.program_id(1), pl.program_id(2)

    @pl.when(ki == 0)
    def _():
        _flash_init(m_ref, l_ref, acc_ref)

    def block(diagonal):
        if diagonal:
            half = (t // 2, t // 2)
            allowed = lax.broadcasted_iota(jnp.int32, half, 1) <= lax.broadcasted_iota(jnp.int32, half, 0)
        for h in range(FOX_HEADS):
            sl = slice(h * LANES, (h + 1) * LANES)
            q_of = lambda r, sl=sl: jnp.concatenate([q_ref[r, sl], aq_ref[r, sl]], axis=1)
            k_of = lambda r, sl=sl: jnp.concatenate([k_ref[r, sl], ak_ref[r, sl]], axis=1)
            v_of = lambda r, sl=sl: v_ref[r, sl]
            if diagonal:
                _flash_diagonal(q_of, k_of, v_of, allowed, t, m_ref, l_ref, acc_ref, h, sl)
            else:
                rows = slice(0, t)
                _flash_step(_qk(q_of(rows), k_of(rows)), v_of(rows), m_ref, l_ref, acc_ref, h, sl)

    @pl.when(ki < qi)
    def _():
        block(False)

    @pl.when(ki == qi)
    def _():
        block(True)
        _flash_finish(o_ref, l_ref, acc_ref, FOX_HEADS)


def _fox_attn_prompt(q, kv, aq, ak, *, batch, seq, t=512):
    w = FOX_HEADS * HEAD_DIM
    nt = seq // t
    assert seq % t == 0
    q_row = lambda b, i, j: b * nt + i
    k_row = lambda b, i, j: b * nt + jnp.minimum(i, j)
    return pl.pallas_call(
        functools.partial(_fox_prompt_body, t=t),
        grid=(batch, nt, nt),
        in_specs=[pl.BlockSpec((t, w), lambda b, i, j: (q_row(b, i, j), 0)),
                  pl.BlockSpec((t, w), lambda b, i, j: (q_row(b, i, j), 0)),
                  pl.BlockSpec((t, w), lambda b, i, j: (k_row(b, i, j), 0)),
                  pl.BlockSpec((t, w), lambda b, i, j: (k_row(b, i, j), 1)),
                  pl.BlockSpec((t, w), lambda b, i, j: (k_row(b, i, j), 0))],
        out_specs=pl.BlockSpec((t, w), lambda b, i, j: (q_row(b, i, j), 0)),
        out_shape=jax.ShapeDtypeStruct((batch * seq, w), BF16),
        scratch_shapes=_flash_scratch(t, FOX_HEADS),
        compiler_params=_cparams(("parallel", "parallel", "arbitrary")),
        name="fox_attn_prompt",
    )(q, aq, kv, kv, ak)


def _fox_sample_body(q_ref, kn_ref, vn_ref, kc_ref, vc_ref, cq_ref, ck_ref, o_ref, *, past, tq):
    allowed = (lax.broadcasted_iota(jnp.int32, (tq, tq), 1)
               <= lax.broadcasted_iota(jnp.int32, (tq, tq), 0))
    for h in range(FOX_HEADS):
        sl = slice(h * HEAD_DIM, (h + 1) * HEAD_DIM)
        q = q_ref[:, sl]
        cq = cq_ref[0, :, h:h + 1] * LOG2E
        ck = ck_ref[0, h:h + 1, :] * LOG2E
        kc = kc_ref[0, 0, pl.ds(h, past, stride=FOX_HEADS), :].astype(BF16)
        vc = vc_ref[0, 0, pl.ds(h, past, stride=FOX_HEADS), :].astype(BF16)
        s_c = _qk(q, kc) + (cq - ck[:, :past])
        s_n = _qk(q, kn_ref[:, sl]) + (cq - ck[:, past:])
        s_n = jnp.where(allowed, s_n, NEG_BIG)
        m = jnp.maximum(jnp.max(s_c, axis=1, keepdims=True), jnp.max(s_n, axis=1, keepdims=True))
        p_c = jnp.exp2(s_c - m)
        p_n = jnp.exp2(s_n - m)
        l = jnp.sum(p_c, axis=1, keepdims=True) + jnp.sum(p_n, axis=1, keepdims=True)
        o = (jnp.dot(p_c.astype(BF16), vc, preferred_element_type=F32)
             + jnp.dot(p_n.astype(BF16), vn_ref[:, sl], preferred_element_type=F32))
        o_ref[:, sl] = (o / l).astype(o_ref.dtype)


def _fox_attn_sample(q, kv, cache_k, cache_v, layer, csum_q, csum_t, *, row0, batch, tq):
    w = FOX_HEADS * HEAD_DIM
    n_layers, n_streams, past = cache_k.shape[:3]
    rb0 = row0 // tq
    flat = (n_layers, n_streams, past * FOX_HEADS, HEAD_DIM)
    cache_spec = pl.BlockSpec((1, 1, past * FOX_HEADS, HEAD_DIM), lambda b: (layer, b, 0, 0))
    return pl.pallas_call(
        functools.partial(_fox_sample_body, past=past, tq=tq),
        grid=(batch,),
        in_specs=[pl.BlockSpec((tq, w), lambda b: (rb0 + b, 0)),
                  pl.BlockSpec((tq, w), lambda b: (rb0 + b, 0)),
                  pl.BlockSpec((tq, w), lambda b: (rb0 + b, 1)),
                  cache_spec, cache_spec,
                  pl.BlockSpec((1, tq, FOX_HEADS), lambda b: (b, 0, 0)),
                  pl.BlockSpec((1, FOX_HEADS, past + tq), lambda b: (b, 0, 0))],
        out_specs=pl.BlockSpec((tq, w), lambda b: (b, 0)),
        out_shape=jax.ShapeDtypeStruct((batch * tq, w), BF16),
        compiler_params=_cparams(("parallel",)),
        name="fox_attn_sample",
    )(q, kv, kv, cache_k.reshape(flat), cache_v.reshape(flat), csum_q, csum_t)


def _rope_body(x_ref, cos_ref, sin_ref, *out_refs, width):
    cos, sin = cos_ref[...], sin_ref[...]
    for h in range(width // LANES):
        a = x_ref[:, h * LANES:(h + 1) * LANES]
        b = x_ref[:, width + h * LANES:width + (h + 1) * LANES]
        y = a * cos + b * sin
        for o in out_refs:
            o[:, h * LANES:(h + 1) * LANES] = y.astype(o.dtype)


def _rope(x, cos, sin, *, x_col=0, width, out_dtypes, tm=512):
    n = x.shape[0]
    assert n % tm == 0
    outs = pl.pallas_call(
        functools.partial(_rope_body, width=width),
        grid=(n // tm,),
        in_specs=[pl.BlockSpec((tm, 2 * width), lambda i: (i, x_col)),
                  pl.BlockSpec((tm, LANES), lambda i: (i, 0)),
                  pl.BlockSpec((tm, LANES), lambda i: (i, 0))],
        out_specs=[pl.BlockSpec((tm, width), lambda i: (i, 0)) for _ in out_dtypes],
        out_shape=[jax.ShapeDtypeStruct((n, width), dt) for dt in out_dtypes],
        compiler_params=_cparams(("parallel",)),
        name="rope",
    )(x, cos, sin)
    return outs[0] if len(out_dtypes) == 1 else outs


def _rope_tables(pos):
    half = MLA_ROPE // 2
    inv_freq = ROPE_BASE ** (-np.arange(half, dtype=np.float32) / half)
    ang = jnp.asarray(pos, F32)[:, None] * jnp.asarray(inv_freq)[None, :]
    zeros = jnp.zeros((ang.shape[0], LANES - MLA_ROPE), F32)
    cos = jnp.concatenate([jnp.cos(ang), jnp.cos(ang), zeros], axis=1)
    sin = jnp.concatenate([jnp.sin(ang), jnp.sin(ang), zeros], axis=1)
    return cos, sin


def _swap_halves(w):
    half = w.shape[-1] // 2
    return jnp.concatenate([-w[..., half:], w[..., :half]], axis=-1)


def _pad_lanes(w):
    return jnp.concatenate([w, jnp.zeros(w.shape[:-1] + (LANES - w.shape[-1],), w.dtype)], axis=-1)


def _mla_prompt_body(qn_ref, qr_ref, kn_ref, v_ref, kr_ref, o_ref, m_ref, l_ref, acc_ref, *, t):
    qi, ki = pl.program_id(1), pl.program_id(2)

    @pl.when(ki == 0)
    def _():
        _flash_init(m_ref, l_ref, acc_ref)

    def block(diagonal):
        if diagonal:
            half = (t // 2, t // 2)
            allowed = ((lax.broadcasted_iota(jnp.int32, half, 1) // CHUNK)
                       <= (lax.broadcasted_iota(jnp.int32, half, 0) // CHUNK))
        for h in range(MLA_HEADS):
            sl = slice(h * LANES, (h + 1) * LANES)
            q_of = lambda r, sl=sl: jnp.concatenate([qn_ref[r, sl], qr_ref[r, sl]], axis=1)
            k_of = lambda r, sl=sl: jnp.concatenate([kn_ref[r, sl], kr_ref[r, :]], axis=1)
            v_of = lambda r, sl=sl: v_ref[r, sl]
            if diagonal:
                _flash_diagonal(q_of, k_of, v_of, allowed, t, m_ref, l_ref, acc_ref, h, sl)
            else:
                rows = slice(0, t)
                _flash_step(_qk(q_of(rows), k_of(rows)), v_of(rows), m_ref, l_ref, acc_ref, h, sl)

    @pl.when(ki < qi)
    def _():
        block(False)

    @pl.when(ki == qi)
    def _():
        block(True)
        _flash_finish(o_ref, l_ref, acc_ref, MLA_HEADS)


def _mla_attn_prompt(qn, qr, kv, kr, *, batch, seq, t=512):
    w = MLA_HEADS * LANES
    nt = seq // t
    assert seq % t == 0 and (t // 2) % CHUNK == 0
    kv_row = lambda b, i, j: b * nt + jnp.minimum(i, j)
    return pl.pallas_call(
        functools.partial(_mla_prompt_body, t=t),
        grid=(batch, nt, nt),
        in_specs=[pl.BlockSpec((t, w), lambda b, i, j: (b * nt + i, 0)),
                  pl.BlockSpec((t, w), lambda b, i, j: (b * nt + i, 0)),
                  pl.BlockSpec((t, w), lambda b, i, j: (kv_row(b, i, j), 0)),
                  pl.BlockSpec((t, w), lambda b, i, j: (kv_row(b, i, j), 1)),
                  pl.BlockSpec((t, LANES), lambda b, i, j: (kv_row(b, i, j), 0))],
        out_specs=pl.BlockSpec((t, w), lambda b, i, j: (b * nt + i, 0)),
        out_shape=jax.ShapeDtypeStruct((batch * seq, w), BF16),
        scratch_shapes=_flash_scratch(t, MLA_HEADS),
        compiler_params=_cparams(("parallel", "parallel", "arbitrary")),
        name="mla_attn_prompt",
    )(qn, qr, kv, kv, kr)


def _mla_sample_body(qn_ref, qr_ref, cn_ref, krn_ref, cc_ref, krc_ref, wuk_ref, wuv_ref, o_ref,
                     *, past, tq):
    heads = MLA_HEADS
    q_lat = jnp.concatenate(
        [jnp.dot(qn_ref[:, h * LANES:(h + 1) * LANES], wuk_ref[h], preferred_element_type=F32)
         for h in range(heads)], axis=0).astype(BF16)
    q_r = jnp.concatenate([qr_ref[:, h * LANES:(h + 1) * LANES] for h in range(heads)], axis=0)
    c_c = cc_ref[0, 0].astype(BF16)
    kr_c = krc_ref[0, 0].astype(BF16)
    c_n = cn_ref[...]
    s_c = _qk(q_lat, c_c) + _qk(q_r[:, :MLA_ROPE], kr_c)
    s_n = _qk(q_lat, c_n) + _qk(q_r, krn_ref[...])
    vis_c = (np.arange(past)[None, :] // CHUNK) <= ((past + np.arange(tq))[:, None] // CHUNK)
    vis_n = ((past + np.arange(tq))[None, :] // CHUNK) <= ((past + np.arange(tq))[:, None] // CHUNK)
    if not vis_c.all():
        frame = lax.broadcasted_iota(jnp.int32, s_c.shape, 0) % tq
        key = lax.broadcasted_iota(jnp.int32, s_c.shape, 1)
        s_c = jnp.where((key // CHUNK) <= ((past + frame) // CHUNK), s_c, NEG_BIG)
    if not vis_n.all():
        frame = lax.broadcasted_iota(jnp.int32, s_n.shape, 0) % tq
        key = lax.broadcasted_iota(jnp.int32, s_n.shape, 1)
        s_n = jnp.where(((past + key) // CHUNK) <= ((past + frame) // CHUNK), s_n, NEG_BIG)
    m = jnp.maximum(jnp.max(s_c, axis=1, keepdims=True), jnp.max(s_n, axis=1, keepdims=True))
    p_c = jnp.exp2(s_c - m)
    p_n = jnp.exp2(s_n - m)
    l = jnp.sum(p_c, axis=1, keepdims=True) + jnp.sum(p_n, axis=1, keepdims=True)
    o_lat = (jnp.dot(p_c.astype(BF16), c_c, preferred_element_type=F32)
             + jnp.dot(p_n.astype(BF16), c_n, preferred_element_type=F32)) / l
    o_lat = o_lat.astype(BF16)
    for h in range(heads):
        o_ref[:, h * LANES:(h + 1) * LANES] = jnp.dot(
            o_lat[h * tq:(h + 1) * tq], wuv_ref[h], preferred_element_type=F32).astype(o_ref.dtype)


def _mla_attn_sample(qn, qr, ckv_new, kr_new, cache_ckv, cache_kr, layer, w_uk_t, w_uv, *, row0, batch, tq):
    w = MLA_HEADS * LANES
    past = cache_ckv.shape[2]
    rb0 = row0 // tq
    return pl.pallas_call(
        functools.partial(_mla_sample_body, past=past, tq=tq),
        grid=(batch,),
        in_specs=[pl.BlockSpec((tq, w), lambda b: (rb0 + b, 0)),
                  pl.BlockSpec((tq, w), lambda b: (rb0 + b, 0)),
                  pl.BlockSpec((tq, MLA_KV_LORA), lambda b: (rb0 + b, 0)),
                  pl.BlockSpec((tq, LANES), lambda b: (rb0 + b, 0)),
                  pl.BlockSpec((1, 1, past, MLA_KV_LORA), lambda b: (layer, b, 0, 0)),
                  pl.BlockSpec((1, 1, past, MLA_ROPE), lambda b: (layer, b, 0, 0)),
                  pl.BlockSpec((MLA_HEADS, MLA_NOPE, MLA_KV_LORA), lambda b: (0, 0, 0)),
                  pl.BlockSpec((MLA_HEADS, MLA_KV_LORA, MLA_V), lambda b: (0, 0, 0))],
        out_specs=pl.BlockSpec((tq, w), lambda b: (b, 0)),
        out_shape=jax.ShapeDtypeStruct((batch * tq, w), BF16),
        compiler_params=_cparams(("parallel",)),
        name="mla_attn_sample",
    )(qn, qr, ckv_new, kr_new, cache_ckv, cache_kr, w_uk_t, w_uv)


def _cross_body(q_ref, k_ref, v_ref, o_ref, *, groups, tq, n_mem, interleaved):
    for g in range(groups):
        rows = slice(g * tq, (g + 1) * tq)
        for h in range(MEM_HEADS):
            sl = slice(h * HEAD_DIM, (h + 1) * HEAD_DIM)
            if interleaved:
                k = k_ref[g, pl.ds(h, n_mem, stride=MEM_HEADS), :]
                v = v_ref[g, pl.ds(h, n_mem, stride=MEM_HEADS), :]
            else:
                k, v = k_ref[g, :, sl], v_ref[g, :, sl]
            s = _qk(q_ref[rows, sl], k.astype(BF16))
            p = jnp.exp2(s - jnp.max(s, axis=1, keepdims=True))
            l = jnp.sum(p, axis=1, keepdims=True)
            o = jnp.dot(p.astype(BF16), v.astype(BF16), preferred_element_type=F32)
            o_ref[rows, sl] = (o / l).astype(o_ref.dtype)


def _cross_attn(q, mem_k, mem_v, *, layer=None, row0, n_rows, rows_per_stream, groups, tq):
    wq = MEM_HEADS * HEAD_DIM
    step_rows = groups * tq
    stream_block = lambda i: i // (rows_per_stream // tq)
    rb0 = row0 // step_rows
    interleaved = mem_v is not None
    if interleaved:
        n_layers, n_streams, n_mem = mem_k.shape[:3]
        flat = (n_layers, n_streams, n_mem * MEM_HEADS, HEAD_DIM)
        spec = pl.BlockSpec((None, groups, n_mem * MEM_HEADS, HEAD_DIM), lambda i: (layer, stream_block(i), 0, 0))
        mem_specs, mem_args = [spec, spec], [mem_k.reshape(flat), mem_v.reshape(flat)]
    else:
        n_mem = mem_k.shape[1]
        mem_specs = [pl.BlockSpec((groups, n_mem, wq), lambda i: (stream_block(i), 0, 0)),
                     pl.BlockSpec((groups, n_mem, wq), lambda i: (stream_block(i), 0, 1))]
        mem_args = [mem_k, mem_k]
    return pl.pallas_call(
        functools.partial(_cross_body, groups=groups, tq=tq, n_mem=n_mem, interleaved=interleaved),
        grid=(n_rows // step_rows,),
        in_specs=[pl.BlockSpec((step_rows, wq), lambda i: (rb0 + i, 0))] + mem_specs,
        out_specs=pl.BlockSpec((step_rows, wq), lambda i: (i, 0)),
        out_shape=jax.ShapeDtypeStruct((n_rows, wq), BF16),
        compiler_params=_cparams(("parallel",)),
        name="cross_attn",
    )(q, *mem_args)


def _split_bf16(x):
    hi = x.astype(BF16)
    return hi, (x - hi.astype(F32)).astype(BF16)


def _router_body(x_ref, g_ref, w_ref, h_ref, idx_ref, gate_ref):
    h = _rms(x_ref[...], g_ref[...])
    h_ref[...] = h
    h_hi, h_lo = _split_bf16(h)
    w_hi, w_lo = _split_bf16(w_ref[...])
    logits = (jnp.dot(h_hi, w_hi, preferred_element_type=F32)
              + jnp.dot(h_hi, w_lo, preferred_element_type=F32)
              + jnp.dot(h_lo, w_hi, preferred_element_type=F32))
    lane = lax.broadcasted_iota(jnp.int32, logits.shape, 1)
    logits = jnp.where(lane < N_EXPERTS, logits, NEG_BIG)
    v1 = jnp.max(logits, axis=1, keepdims=True)
    i1 = jnp.min(jnp.where(logits == v1, lane, LANES), axis=1, keepdims=True)
    rest = jnp.where(lane == i1, NEG_BIG, logits)
    v2 = jnp.max(rest, axis=1, keepdims=True)
    i2 = jnp.min(jnp.where(rest == v2, lane, LANES), axis=1, keepdims=True)
    e2 = jnp.exp(v2 - v1)
    g1 = 1.0 / (1.0 + e2)
    g2 = e2 / (1.0 + e2)
    idx_ref[...] = jnp.where(lane == 0, i1, jnp.where(lane == 1, i2, 0))
    gate_ref[...] = jnp.where(lane == 0, g1, jnp.where(lane == 1, g2, 0.0))


def _router(x, gain, w_router, *, tm=512):
    n, d = x.shape
    assert n % tm == 0
    return pl.pallas_call(
        _router_body,
        grid=(n // tm,),
        in_specs=[pl.BlockSpec((tm, d), lambda i: (i, 0)),
                  pl.BlockSpec((1, d), lambda i: (0, 0)),
                  pl.BlockSpec((d, LANES), lambda i: (0, 0))],
        out_specs=[pl.BlockSpec((tm, d), lambda i: (i, 0)),
                   pl.BlockSpec((tm, LANES), lambda i: (i, 0)),
                   pl.BlockSpec((tm, LANES), lambda i: (i, 0))],
        out_shape=[jax.ShapeDtypeStruct((n, d), F32),
                   jax.ShapeDtypeStruct((n, LANES), jnp.int32),
                   jax.ShapeDtypeStruct((n, LANES), F32)],
        compiler_params=_cparams(("parallel",)),
        name="router",
    )(x, gain.reshape(1, d).astype(F32), _pad_lanes(w_router))


ROW_DMA_UNROLL = 8


def _gather_body(idx_ref, src_ref, out_ref, buf_ref, sem, *, rows, n_steps):
    i = pl.program_id(0)

    def issue_block(blk):
        slot = blk % 2
        base = blk * rows

        def issue(g, carry):
            for u in range(ROW_DMA_UNROLL):
                r = g * ROW_DMA_UNROLL + u
                pltpu.make_async_copy(src_ref.at[pl.ds(idx_ref[base + r], 1)],
                                      buf_ref.at[slot, pl.ds(r, 1)], sem.at[slot]).start(priority=u % 2)
            return carry

        lax.fori_loop(0, rows // ROW_DMA_UNROLL, issue, 0)

    @pl.when(i == 0)
    def _():
        issue_block(i)

    @pl.when(i + 1 < n_steps)
    def _():
        issue_block(i + 1)

    slot = i % 2
    pltpu.make_async_copy(src_ref.at[pl.ds(0, rows)], buf_ref.at[slot], sem.at[slot]).wait()
    out_ref[...] = buf_ref[slot].astype(out_ref.dtype)


def _gather_rows(src, row_idx, *, out_dtype, rows=512):
    r, d = row_idx.shape[0], src.shape[1]
    assert r % rows == 0 and rows % ROW_DMA_UNROLL == 0 and src.dtype == F32
    return pl.pallas_call(
        functools.partial(_gather_body, rows=rows, n_steps=r // rows),
        grid_spec=pltpu.PrefetchScalarGridSpec(
            num_scalar_prefetch=1, grid=(r // rows,),
            in_specs=[pl.BlockSpec(memory_space=pl.ANY)],
            out_specs=pl.BlockSpec((rows, d), lambda i, idx: (i, 0)),
            scratch_shapes=[pltpu.VMEM((2, rows, d), F32), pltpu.SemaphoreType.DMA((2,))]),
        out_shape=jax.ShapeDtypeStruct((r, d), out_dtype),
        compiler_params=_cparams(("arbitrary",)),
        name="gather_rows",
    )(row_idx, src)


def _moe_body(wt_ref, we_ref, ws_ref, wc_ref, wf_ref, x_ref, wg_ref, wu_ref, wd_ref, out_ref, *, ts, nsub):
    w, f = pl.program_id(0), pl.program_id(1)

    @pl.when((f == 0) & (wf_ref[w] == 1))
    def _():
        out_ref[...] = jnp.zeros(out_ref.shape, F32)

    s0, cnt = ws_ref[w], wc_ref[w]
    for a in range(nsub):
        for c in range(1, nsub - a + 1):
            @pl.when((s0 == a) & (cnt == c))
            def _():
                rows = slice(a * ts, (a + c) * ts)
                out_ref[rows, :] += _swiglu_partial(x_ref[rows, :], wg_ref[...], wu_ref[...], wd_ref[...])


def _moe_ffn(xs, w_gu, w_d, layer, items, *, tm, ts, tf=512):
    r, d = xs.shape
    f = w_d.shape[2]
    nf = f // tf
    n_items = items[0].shape[0]
    assert r % tm == 0 and tm % ts == 0 and f % tf == 0

    def fidx(w, j, wc):
        return jnp.where(wc[w] > 0, j, nf - 1)

    return pl.pallas_call(
        functools.partial(_moe_body, ts=ts, nsub=tm // ts),
        grid_spec=pltpu.PrefetchScalarGridSpec(
            num_scalar_prefetch=5, grid=(n_items, nf),
            in_specs=[pl.BlockSpec((tm, d), lambda w, j, wt, we, ws, wc, wf: (wt[w], 0)),
                      pl.BlockSpec((None, None, d, tf),
                                   lambda w, j, wt, we, ws, wc, wf: (layer, we[w], 0, fidx(w, j, wc))),
                      pl.BlockSpec((None, None, d, tf),
                                   lambda w, j, wt, we, ws, wc, wf: (layer, we[w], 0, nf + fidx(w, j, wc))),
                      pl.BlockSpec((None, None, tf, d),
                                   lambda w, j, wt, we, ws, wc, wf: (layer, we[w], fidx(w, j, wc), 0))],
            out_specs=pl.BlockSpec((tm, d), lambda w, j, wt, we, ws, wc, wf: (wt[w], 0))),
        out_shape=jax.ShapeDtypeStruct((r, d), F32),
        compiler_params=_cparams(("arbitrary", "arbitrary")),
        name="moe_ffn",
    )(*items, xs, w_gu, w_gu, w_d)


def _combine_body(p0_ref, p1_ref, x_ref, gate_ref, ys_ref, out_ref, buf_ref, sem, *, rows, n_steps):
    i = pl.program_id(0)

    def issue_block(blk):
        slot = blk % 2
        base = blk * rows

        def issue(g, carry):
            for u in range(ROW_DMA_UNROLL):
                r = g * ROW_DMA_UNROLL + u
                pltpu.make_async_copy(ys_ref.at[pl.ds(p0_ref[base + r], 1)],
                                      buf_ref.at[slot, 0, pl.ds(r, 1)], sem.at[slot]).start(priority=0)
                pltpu.make_async_copy(ys_ref.at[pl.ds(p1_ref[base + r], 1)],
                                      buf_ref.at[slot, 1, pl.ds(r, 1)], sem.at[slot]).start(priority=1)
            return carry

        lax.fori_loop(0, rows // ROW_DMA_UNROLL, issue, 0)

    @pl.when(i == 0)
    def _():
        issue_block(i)

    @pl.when(i + 1 < n_steps)
    def _():
        issue_block(i + 1)

    slot = i % 2
    pltpu.make_async_copy(ys_ref.at[pl.ds(0, rows)], buf_ref.at[slot, 0], sem.at[slot]).wait()
    pltpu.make_async_copy(ys_ref.at[pl.ds(0, rows)], buf_ref.at[slot, 1], sem.at[slot]).wait()
    g = gate_ref[...]
    out_ref[...] = x_ref[...] + (g[:, 0:1] * buf_ref[slot, 0] + g[:, 1:2] * buf_ref[slot, 1])


def _moe_combine(x, gates, ys, pos0, pos1, *, rows=256):
    n, d = x.shape
    assert n % rows == 0 and rows % ROW_DMA_UNROLL == 0
    return pl.pallas_call(
        functools.partial(_combine_body, rows=rows, n_steps=n // rows),
        grid_spec=pltpu.PrefetchScalarGridSpec(
            num_scalar_prefetch=2, grid=(n // rows,),
            in_specs=[pl.BlockSpec((rows, d), lambda i, p0, p1: (i, 0)),
                      pl.BlockSpec((rows, LANES), lambda i, p0, p1: (i, 0)),
                      pl.BlockSpec(memory_space=pl.ANY)],
            out_specs=pl.BlockSpec((rows, d), lambda i, p0, p1: (i, 0)),
            scratch_shapes=[pltpu.VMEM((2, 2, rows, d), F32), pltpu.SemaphoreType.DMA((2,))]),
        out_shape=jax.ShapeDtypeStruct((n, d), F32),
        compiler_params=_cparams(("arbitrary",)),
        name="moe_combine",
    )(pos0, pos1, x, gates, ys)


def _moe_plan(idx, *, tm, ts):
    n = idx.shape[0]
    nsub = tm // ts
    cap = -(-(2 * n + N_EXPERTS * ts) // tm) * tm
    n_tiles, n_subs = cap // tm, cap // ts
    n_items = n_tiles + N_EXPERTS - 1
    e_flat = idx.reshape(-1)
    onehot = (e_flat[:, None] == jnp.arange(N_EXPERTS, dtype=jnp.int32)[None, :]).astype(jnp.int32)
    before = jnp.cumsum(onehot, axis=0) - onehot
    rank = jnp.sum(before * onehot, axis=1)
    counts = jnp.sum(onehot, axis=0)
    padded = ((counts + ts - 1) // ts) * ts
    g_end = jnp.cumsum(padded)
    g_start = g_end - padded
    pos = g_start[e_flat] + rank
    row_token = jnp.zeros((cap,), jnp.int32).at[pos].set(jnp.arange(2 * n, dtype=jnp.int32) // 2)
    s = jnp.arange(n_subs, dtype=jnp.int32)
    sub_e = jnp.sum((s[:, None] * ts >= g_end[None, :]).astype(jnp.int32), axis=1)
    valid = sub_e < N_EXPERTS
    prev_e = jnp.concatenate([jnp.full((1,), -1, jnp.int32), sub_e[:-1]])
    tile_start = (s % nsub) == 0
    new = tile_start | (valid & (sub_e != prev_e))
    item_of_sub = jnp.cumsum(new.astype(jnp.int32)) - 1
    n_used = item_of_sub[-1] + 1
    tgt = jnp.where(new, item_of_sub, n_items)
    w = jnp.arange(n_items, dtype=jnp.int32)
    used = w < n_used
    it_tile = jnp.full((n_items,), n_tiles - 1, jnp.int32).at[tgt].set(s // nsub, mode="drop")
    it_exp = jnp.full((n_items,), N_EXPERTS - 1, jnp.int32).at[tgt].set(
        jnp.minimum(sub_e, N_EXPERTS - 1), mode="drop")
    it_s0 = jnp.zeros((n_items,), jnp.int32).at[tgt].set(s % nsub, mode="drop")
    it_first = jnp.zeros((n_items,), jnp.int32).at[tgt].set(tile_start.astype(jnp.int32), mode="drop")
    it_cnt = jnp.zeros((n_items,), jnp.int32).at[item_of_sub].add(valid.astype(jnp.int32), mode="drop")
    it_cnt = jnp.where(used, it_cnt, 0)
    return pos[0::2], pos[1::2], row_token, (it_tile, it_exp, it_s0, it_cnt, it_first)


def _moe_layer(x, gain, w_router, w_gu, w_d, layer, *, tm=1024, ts=256):
    h, idx, gates = _router(x, gain, w_router)
    pos0, pos1, row_token, items = _moe_plan(idx[:, :2], tm=tm, ts=ts)
    xs = _gather_rows(h, row_token, out_dtype=BF16)
    ys = _moe_ffn(xs, w_gu, w_d, layer, items, tm=tm, ts=ts)
    return _moe_combine(x, gates, ys, pos0, pos1)


def _fox_layer(x, gain, w_in, b_f, w_o, cache_k, cache_v, cache_logf, layer, *, bp, tp, bs, tq):
    n_p = bp * tp
    w = FOX_HEADS * HEAD_DIM
    q_scale = jnp.full((w,), HEAD_DIM ** -0.5 * LOG2E, F32)
    q = _matmul(x, w_in, layer=layer, gain=gain, col_scale=q_scale, m=w, out_dtypes=(BF16,))
    kv, kv_b = _matmul(x, w_in, layer=layer, gain=gain, w_col0=w, m=2 * w, out_dtypes=(F32, BF16))
    logf = _matmul(x, _pad_lanes(w_in[layer, :, 3 * w:]), gain=gain, bias=_pad_lanes(b_f),
                   act="log_sigmoid")[:, :FOX_HEADS]
    logf_p = logf[:n_p].reshape(bp, tp, FOX_HEADS)
    logf_s = logf[n_p:].reshape(bs, tq, FOX_HEADS)
    past = cache_logf.shape[2]
    cs_p = _cumsum_time(logf_p)
    cs_s = _cumsum_time(jnp.concatenate([cache_logf[layer], logf_s], axis=1))
    aq, ak = _fox_aug(cs_p.reshape(n_p, FOX_HEADS))
    o_p = _fox_attn_prompt(q, kv_b, aq, ak, batch=bp, seq=tp)
    o_s = _fox_attn_sample(q, kv_b, cache_k, cache_v, layer, cs_s[:, past:], cs_s.transpose(0, 2, 1),
                           row0=n_p, batch=bs, tq=tq)
    x = _matmul((o_p, o_s), w_o, layer=layer, res=x)
    return x, kv[:, :w], kv[:, w:], logf


def _mla_layer(x, gain, w_in, g_q, g_kv, w_uq, w_ukv, w_o, layer, cache_ckv, cache_kr, cos, sin,
               *, bp, tp, bs, tq):
    n_p = bp * tp
    h, c, r = MLA_HEADS, MLA_KV_LORA, MLA_ROPE
    w_kr = w_in[:, MLA_Q_LORA + c:]
    w_in_big = jnp.concatenate([w_in[:, :MLA_Q_LORA + c], _pad_lanes(w_kr), _pad_lanes(_swap_halves(w_kr))], axis=1)
    w_q = w_uq.reshape(MLA_Q_LORA, h, MLA_NOPE + r)
    w_qr = w_q[:, :, MLA_NOPE:]
    w_uq_big = jnp.concatenate([w_q[:, :, :MLA_NOPE].reshape(MLA_Q_LORA, h * MLA_NOPE),
                                _pad_lanes(w_qr).reshape(MLA_Q_LORA, h * LANES),
                                _pad_lanes(_swap_halves(w_qr)).reshape(MLA_Q_LORA, h * LANES)], axis=1)
    w_kv = w_ukv.reshape(c, h, MLA_NOPE + MLA_V)
    w_ukv_split = jnp.concatenate([w_kv[:, :, :MLA_NOPE].reshape(c, h * MLA_NOPE),
                                   w_kv[:, :, MLA_NOPE:].reshape(c, h * MLA_V)], axis=1)
    w_uk_t = w_kv[:, :, :MLA_NOPE].transpose(1, 2, 0).astype(BF16)
    w_uv = w_kv[:, :, MLA_NOPE:].transpose(1, 0, 2).astype(BF16)
    q_scale = (MLA_NOPE + r) ** -0.5 * LOG2E

    a = _matmul(x, w_in_big, gain=gain, tn=256)
    ckv, ckv_b = _rmsnorm(a, g_kv, x_col=1, k=c, out_dtypes=(F32, BF16))
    kr, kr_b = _rope(a, cos, sin, x_col=(MLA_Q_LORA + c) // (2 * LANES), width=LANES, out_dtypes=(F32, BF16))
    qn = _matmul(a, w_uq_big, gain=g_q, x_col=0, k=MLA_Q_LORA, m=h * MLA_NOPE,
                 col_scale=jnp.full((h * MLA_NOPE,), q_scale, F32), out_dtypes=(BF16,))
    qr2 = _matmul(a, w_uq_big, gain=g_q, x_col=0, k=MLA_Q_LORA, w_col0=h * MLA_NOPE,
                  col_scale=jnp.full((2 * h * LANES,), q_scale, F32))
    qr = _rope(qr2, cos, sin, width=h * LANES, out_dtypes=(BF16,))
    kv = _matmul(ckv_b, w_ukv_split, n_rows=n_p, out_dtypes=(BF16,))
    o_p = _mla_attn_prompt(qn, qr, kv, kr_b, batch=bp, seq=tp)
    o_s = _mla_attn_sample(qn, qr, ckv_b, kr_b, cache_ckv, cache_kr, layer, w_uk_t, w_uv,
                           row0=n_p, batch=bs, tq=tq)
    x = _matmul((o_p, o_s), w_o, layer=layer, res=x)
    return x, ckv, kr[:, :r]


def _cross_layer(x, gain, w_q, w_o, layer, kv_p, cache_k, cache_v, *, bp, tp, bs, tq):
    n_p = bp * tp
    wq = MEM_HEADS * HEAD_DIM
    q = _matmul(x, w_q, layer=layer, gain=gain, col_scale=jnp.full((wq,), HEAD_DIM ** -0.5 * LOG2E, F32),
                out_dtypes=(BF16,))
    o_p = _cross_attn(q, kv_p, None, row0=0, n_rows=n_p, rows_per_stream=tp, groups=1, tq=1024)
    o_s = _cross_attn(q, cache_k, cache_v, layer=layer, row0=n_p, n_rows=bs * tq, rows_per_stream=tq,
                      groups=8, tq=tq)
    return _matmul((o_p, o_s), w_o, layer=layer, res=x)


def kernel(x_prompt, x_sample, cache_fox_k, cache_fox_v, cache_fox_logf, cache_mla_ckv, cache_mla_krope, cache_mem_k, cache_mem_v, mem_prompt, norm_mix, norm_cross, norm_ffn, norm_final, fox_w_in, fox_b_f, fox_w_o, mla_w_in, mla_g_q, mla_g_kv, mla_w_uq, mla_w_ukv, mla_w_o, cross_w_q, cross_w_kv, cross_w_o, ffn_w_gu, ffn_w_d, moe_w_router, moe_w_gu, moe_w_d):
    bp, tp, d = x_prompt.shape
    bs, tq, _ = x_sample.shape
    n_p = bp * tp
    depth = norm_mix.shape[0]
    past = cache_fox_k.shape[2]
    n_mem = mem_prompt.shape[1]
    mem_w = MEM_HEADS * HEAD_DIM
    dims = dict(bp=bp, tp=tp, bs=bs, tq=tq)

    x = jnp.concatenate([x_prompt.reshape(n_p, d), x_sample.reshape(bs * tq, d)], axis=0)
    pos = np.concatenate([np.tile(np.arange(tp), bp), np.tile(past + np.arange(tq), bs)])
    cos, sin = _rope_tables(pos)
    mem_flat = mem_prompt.reshape(bp * n_mem, d)

    fox_rows, mla_rows, mem_rows = [], [], []
    for i in range(depth):
        j = i // 2
        if i % 2 == 0:
            x, k, v, lf = _fox_layer(x, norm_mix[i], fox_w_in, fox_b_f[j], fox_w_o,
                                     cache_fox_k, cache_fox_v, cache_fox_logf, j, **dims)
            fox_rows.append((k, v, lf))
        else:
            x, ckv, kr = _mla_layer(x, norm_mix[i], mla_w_in[j], mla_g_q[j], mla_g_kv[j], mla_w_uq[j],
                                    mla_w_ukv[j], mla_w_o, j, cache_mla_ckv, cache_mla_krope,
                                    cos, sin, **dims)
            mla_rows.append((ckv, kr))
        kv_p = _matmul(mem_flat, cross_w_kv, layer=i).reshape(bp, n_mem, 2 * mem_w)
        mem_rows.append((kv_p[:, :, :mem_w], kv_p[:, :, mem_w:]))
        x = _cross_layer(x, norm_cross[i], cross_w_q, cross_w_o, i, kv_p, cache_mem_k, cache_mem_v, **dims)
        if i % 2 == 0:
            x = _ffn(x, norm_ffn[i], ffn_w_gu, ffn_w_d, j)
        else:
            x = _moe_layer(x, norm_ffn[i], moe_w_router[j], moe_w_gu, moe_w_d, j)
    y_p = _rmsnorm(x, norm_final, n_rows=n_p).reshape(bp, tp, d)
    y_s = _rmsnorm(x, norm_final, row0=n_p).reshape(bs, tq, d)

    def split(rows, tail):
        return rows[:n_p].reshape((bp, tp) + tail), rows[n_p:].reshape((bs, tq) + tail)

    def stacked(rows_list, tail):
        parts = [split(r, tail) for r in rows_list]
        return jnp.stack([p[0] for p in parts]), jnp.stack([p[1] for p in parts])

    hd = (FOX_HEADS, HEAD_DIM)
    fox_k_p, fox_k_s = stacked([r[0] for r in fox_rows], hd)
    fox_v_p, fox_v_s = stacked([r[1] for r in fox_rows], hd)
    fox_lf_p, fox_lf_s = stacked([r[2] for r in fox_rows], (FOX_HEADS,))
    mla_ckv_p, mla_ckv_s = stacked([r[0] for r in mla_rows], (MLA_KV_LORA,))
    mla_kr_p, mla_kr_s = stacked([r[1] for r in mla_rows], (MLA_ROPE,))
    mem_k_p = jnp.stack([m[0].reshape(bp, n_mem, MEM_HEADS, HEAD_DIM) for m in mem_rows])
    mem_v_p = jnp.stack([m[1].reshape(bp, n_mem, MEM_HEADS, HEAD_DIM) for m in mem_rows])
    return (y_p, y_s, fox_k_p, fox_v_p, fox_lf_p, mla_ckv_p, mla_kr_p, mem_k_p, mem_v_p,
            fox_k_s, fox_v_s, fox_lf_s, mla_ckv_s, mla_kr_s)
```
